```python
import math
import jax, jax.numpy as jnp
from jax import lax
import numpy as np

D_MODEL = 2048
BATCH = 16
SEQ = 2048
DEPTH = 1
DEC_BATCH = 8
DEC_SEQ = 2048
PAST_LEN = 128

MIX_WIDTH = D_MODEL
ATTN_HEADS = 8
ATTN_KV_HEADS = 2
HEAD_DIM = 128
ATTN_GROUP = ATTN_HEADS // ATTN_KV_HEADS
ATTN_WIDTH = ATTN_HEADS * HEAD_DIM
ATTN_KV_WIDTH = ATTN_KV_HEADS * HEAD_DIM
WINDOW = 128
ATTN_BLOCK = 128
HG_HEADS = 8
HG_DK = 128
HG_DV = 128
HG_F = HG_HEADS * HG_DK
HG_WIDTH = HG_HEADS * HG_DV
HG_CHUNK = 64
N_GROUPS = 4
EXPERTS_PER_GROUP = 8
N_EXPERTS = N_GROUPS * EXPERTS_PER_GROUP
TOP_K = 2
D_EXPERT = 512
EPS = 1e-6

IN_SIZES = [ATTN_WIDTH, ATTN_KV_WIDTH, ATTN_KV_WIDTH, HG_F, HG_F, HG_F, HG_WIDTH, HG_WIDTH]
IN_WIDTH = sum(IN_SIZES)
SPLIT_POINTS = list(np.cumsum(IN_SIZES)[:-1].tolist())

kernel_name = "hymba_style_swa_hgrn2_hmoe_encoder"


def rmsnorm(x, w):
    xf = x.astype(jnp.float32)
    xf = xf * lax.rsqrt(jnp.mean(xf * xf, axis=-1, keepdims=True) + EPS)
    return xf * w.astype(jnp.float32)


def alibi_slopes():
    return jnp.exp2(-8.0 * jnp.arange(1, ATTN_HEADS + 1, dtype=jnp.float32) / ATTN_HEADS)


def windowed_attention(q, k, v, sink):
    B, S = q.shape[0], q.shape[1]
    nb = S // ATTN_BLOCK
    qb = q.reshape(B, nb, ATTN_BLOCK, ATTN_KV_HEADS, ATTN_GROUP, HEAD_DIM).astype(jnp.float32)
    pad = ((0, 0), (ATTN_BLOCK, ATTN_BLOCK), (0, 0), (0, 0))
    kp = jnp.pad(k, pad).reshape(B, nb + 2, ATTN_BLOCK, ATTN_KV_HEADS, HEAD_DIM)
    vp = jnp.pad(v, pad).reshape(B, nb + 2, ATTN_BLOCK, ATTN_KV_HEADS, HEAD_DIM)
    kw = jnp.concatenate([kp[:, :-2], kp[:, 1:-1], kp[:, 2:]], axis=2).astype(jnp.float32)
    vw = jnp.concatenate([vp[:, :-2], vp[:, 1:-1], vp[:, 2:]], axis=2).astype(jnp.float32)
    scores = jnp.einsum('bnqhgd,bnkhd->bnhgqk', qb, kw) * (HEAD_DIM ** -0.5)
    qi = jnp.arange(ATTN_BLOCK)[:, None]
    kc = jnp.arange(3 * ATTN_BLOCK)[None, :]
    dist = kc - ATTN_BLOCK - qi
    kpos = jnp.arange(nb)[:, None] * ATTN_BLOCK + jnp.arange(3 * ATTN_BLOCK)[None, :] - ATTN_BLOCK
    valid = (kpos >= 0) & (kpos < S)
    mask = (jnp.abs(dist) <= WINDOW)[None] & valid[:, None, :]
    slopes = alibi_slopes().reshape(ATTN_KV_HEADS, ATTN_GROUP, 1, 1)
    bias = -slopes * jnp.abs(dist).astype(jnp.float32)
    s = jnp.where(mask[None, :, None, None], scores + bias, -jnp.inf)
    sink_b = sink.astype(jnp.float32).reshape(ATTN_KV_HEADS, ATTN_GROUP, 1, 1)
    m = jnp.maximum(jnp.max(s, axis=-1, keepdims=True), sink_b)
    p = jnp.exp(s - m)
    denom = jnp.sum(p, axis=-1, keepdims=True) + jnp.exp(sink_b - m)
    out = jnp.einsum('bnhgqk,bnkhd->bnqhgd', p / denom, vw)
    return out.reshape(B, S, ATTN_WIDTH)


def hgrn2_chunk_scan(q, k, v, log_f):
    B, S, H, _ = q.shape
    nc = S // HG_CHUNK

    def to_chunks(a):
        return a.reshape(B, nc, HG_CHUNK, H, a.shape[-1]).transpose(1, 0, 3, 2, 4)

    tri = jnp.tril(jnp.ones((HG_CHUNK, HG_CHUNK), dtype=bool))[:, :, None]

    def step(state, inp):
        qc, kc, vc, gc = inp
        b = jnp.cumsum(gc, axis=2)
        diff = b[:, :, :, None, :] - b[:, :, None, :, :]
        decay = jnp.exp(jnp.where(tri, diff, -jnp.inf))
        a = jnp.einsum('bhtsk,bhsk->bhts', qc[:, :, :, None, :] * decay, kc)
        o = jnp.einsum('bhts,bhsv->bhtv', a, vc) + jnp.einsum('bhtk,bhkv->bhtv', qc * jnp.exp(b), state)
        b_last = b[:, :, -1:, :]
        new_state = jnp.exp(b_last[:, :, 0, :])[..., None] * state + \
            jnp.einsum('bhsk,bhsv->bhkv', kc * jnp.exp(b_last - b), vc)
        return new_state, o

    s0 = jnp.zeros((B, H, HG_DK, HG_DV), jnp.float32)
    _, o = lax.scan(step, s0, (to_chunks(q), to_chunks(k), to_chunks(v), to_chunks(log_f)))
    return o.transpose(1, 0, 3, 2, 4).reshape(B, S, H, HG_DV)


def hgrn2_bidirectional(q_pre, f_fwd_pre, f_bwd_pre, i_in, g_pre, lb, norm_w):
    B, S = q_pre.shape[0], q_pre.shape[1]
    shp = (B, S, HG_HEADS, HG_DK)
    q = jax.nn.silu(q_pre.astype(jnp.float32)).reshape(shp)
    v = i_in.astype(jnp.float32).reshape(B, S, HG_HEADS, HG_DV)

    def gates(z, lb_d):
        z = z.astype(jnp.float32)
        lb_d = lb_d.reshape(HG_HEADS, HG_DK)
        f = lb_d + (1.0 - lb_d) * jax.nn.sigmoid(z.reshape(shp))
        k = (1.0 - lb_d) * jax.nn.sigmoid(-z.reshape(shp))
        return k, jnp.log(f)

    k_f, lf_f = gates(f_fwd_pre, lb[0])
    k_b, lf_b = gates(f_bwd_pre, lb[1])
    o_f = hgrn2_chunk_scan(q, k_f, v, lf_f)
    flip = lambda a: jnp.flip(a, axis=1)
    o_b = flip(hgrn2_chunk_scan(flip(q), flip(k_b), flip(v), flip(lf_b)))
    o = o_f + o_b
    o = rmsnorm(o, norm_w)
    g = jax.nn.silu(g_pre.astype(jnp.float32)).reshape(B, S, HG_HEADS, HG_DV)
    return (o * g).reshape(B, S, HG_WIDTH)


def hier_moe(h, w_rg, b_rg, w_re, b_re, w_gate, w_up, w_down):
    T = h.shape[0]
    gp = jax.nn.softmax((h @ w_rg + b_rg).astype(jnp.float32), axis=-1)
    g_idx = jnp.argmax(gp, axis=-1)
    g_w = jnp.take_along_axis(gp, g_idx[:, None], axis=-1)
    el = jnp.einsum('td,dge->tge', h, w_re) + b_re
    el = jnp.take_along_axis(el, g_idx[:, None, None], axis=1)[:, 0]
    ep = jax.nn.softmax(el.astype(jnp.float32), axis=-1)
    top_w, top_i = lax.top_k(ep, TOP_K)
    top_w = top_w / jnp.sum(top_w, axis=-1, keepdims=True)
    e_global = g_idx[:, None] * EXPERTS_PER_GROUP + top_i
    combine = jnp.sum(jax.nn.one_hot(e_global, N_EXPERTS, dtype=jnp.float32)
                      * (g_w * top_w)[..., None], axis=1)
    combine = combine.astype(h.dtype)
    y = jnp.zeros((T, D_MODEL), h.dtype)
    for e in range(N_EXPERTS):
        a = jax.nn.silu(h @ w_gate[e]) * (h @ w_up[e])
        y = y + combine[:, e:e + 1] * (a @ w_down[e])
    return y


def trunk(x, norm1_w, w_in, attn_sink, hg_lb, hg_norm_w, w_out, norm2_w,
          w_router_g, b_router_g, w_router_e, b_router_e, w_gate, w_up, w_down, final_norm_w):
    dt = x.dtype
    B, S, _ = x.shape
    lb_all = jnp.cumsum(jax.nn.softmax(hg_lb.astype(jnp.float32), axis=0), axis=0)
    for l in range(DEPTH):
        h = rmsnorm(x, norm1_w[l]).astype(dt)
        proj = h @ w_in[l]
        qa, ka, va, qh, ffw, fbw, ih, gh = jnp.split(proj, SPLIT_POINTS, axis=-1)
        attn_out = windowed_attention(
            qa.reshape(B, S, ATTN_HEADS, HEAD_DIM),
            ka.reshape(B, S, ATTN_KV_HEADS, HEAD_DIM),
            va.reshape(B, S, ATTN_KV_HEADS, HEAD_DIM), attn_sink[l])
        hg_out = hgrn2_bidirectional(qh, ffw, fbw, ih, gh, lb_all[l], hg_norm_w[l])
        mix = jnp.concatenate([attn_out, hg_out], axis=-1).astype(dt)
        x = x + mix @ w_out[l]
        h2 = rmsnorm(x, norm2_w[l]).astype(dt).reshape(B * S, D_MODEL)
        x = x + hier_moe(h2, w_router_g[l], b_router_g[l], w_router_e[l], b_router_e[l],
                         w_gate[l], w_up[l], w_down[l]).reshape(B, S, D_MODEL)
    return rmsnorm(x, final_norm_w).astype(dt)


def setup_inputs(seed: int = 0) -> dict:
    key = jax.random.key(seed)
    ks = jax.random.split(key, 20)
    f32 = jnp.float32
    nrm = lambda k, shp, s: jax.random.normal(k, shp, f32) * s
    return {
        "x_prompt": nrm(ks[0], (BATCH, SEQ, D_MODEL), 1.0),
        "x_sample": nrm(ks[1], (DEC_BATCH, DEC_SEQ, D_MODEL), 1.0),
        "norm1_w": 1.0 + nrm(ks[2], (DEPTH, D_MODEL), 0.02),
        "w_in": nrm(ks[3], (DEPTH, D_MODEL, IN_WIDTH), D_MODEL ** -0.5),
        "attn_sink": nrm(ks[4], (DEPTH, ATTN_HEADS), 0.5),
        "hg_lb": nrm(ks[5], (DEPTH + 1, 2, HG_F), 0.5),
        "hg_norm_w": 1.0 + nrm(ks[6], (DEPTH, HG_DV), 0.02),
        "w_out": nrm(ks[7], (DEPTH, MIX_WIDTH, D_MODEL), MIX_WIDTH ** -0.5),
        "norm2_w": 1.0 + nrm(ks[8], (DEPTH, D_MODEL), 0.02),
        "w_router_g": nrm(ks[9], (DEPTH, D_MODEL, N_GROUPS), D_MODEL ** -0.5),
        "b_router_g": nrm(ks[10], (DEPTH, N_GROUPS), 0.01),
        "w_router_e": nrm(ks[11], (DEPTH, D_MODEL, N_GROUPS, EXPERTS_PER_GROUP), D_MODEL ** -0.5),
        "b_router_e": nrm(ks[12], (DEPTH, N_GROUPS, EXPERTS_PER_GROUP), 0.01),
        "w_gate": nrm(ks[13], (DEPTH, N_EXPERTS, D_MODEL, D_EXPERT), D_MODEL ** -0.5),
        "w_up": nrm(ks[14], (DEPTH, N_EXPERTS, D_MODEL, D_EXPERT), D_MODEL ** -0.5),
        "w_down": nrm(ks[15], (DEPTH, N_EXPERTS, D_EXPERT, D_MODEL), D_EXPERT ** -0.5),
        "final_norm_w": 1.0 + nrm(ks[16], (D_MODEL,), 0.02),
    }


def reference(x_prompt, x_sample, norm1_w, w_in, attn_sink, hg_lb, hg_norm_w, w_out, norm2_w,
              w_router_g, b_router_g, w_router_e, b_router_e, w_gate, w_up, w_down, final_norm_w):
    y_prompt = trunk(x_prompt, norm1_w, w_in, attn_sink, hg_lb, hg_norm_w, w_out, norm2_w,
                     w_router_g, b_router_g, w_router_e, b_router_e, w_gate, w_up, w_down, final_norm_w)
    y_sample = trunk(x_sample, norm1_w, w_in, attn_sink, hg_lb, hg_norm_w, w_out, norm2_w,
                     w_router_g, b_router_g, w_router_e, b_router_e, w_gate, w_up, w_down, final_norm_w)
    return (y_prompt, y_sample)
```

```python
import functools

import numpy as np
import jax
import jax.numpy as jnp
from jax import lax
from jax.experimental import pallas as pl
from jax.experimental.pallas import tpu as pltpu

F32 = jnp.float32
BF16 = jnp.bfloat16

LANE = 128
D_MODEL = 2048
ATTN_HEADS = 8
ATTN_KV_HEADS = 2
ATTN_GROUP = ATTN_HEADS // ATTN_KV_HEADS
HEAD_DIM = LANE
WINDOW = 128
ATTN_BLOCK = 128
HG_HEADS = 8
HG_CHUNK = 128
N_GROUPS = 4
EXPERTS_PER_GROUP = 8
N_EXPERTS = N_GROUPS * EXPERTS_PER_GROUP
D_EXPERT = 512
EPS = 1e-6

SLAB_QA, SLAB_KA, SLAB_VA, SLAB_QH, SLAB_FF, SLAB_FB, SLAB_IH, SLAB_GH = 0, 8, 10, 12, 20, 28, 36, 44
N_SLABS = 52

EXPERT_TILE = 256
VMEM_LIMIT = 56 * 1024 * 1024


def _largest_divisor(n, cap):
    n = int(n)
    d = min(n, cap)
    while n % d:
        d -= 1
    return d


def _cparams(*sem):
    return pltpu.CompilerParams(dimension_semantics=sem, vmem_limit_bytes=VMEM_LIMIT)


def _inproj_kernel(nbp, xp_ref, xs_ref, nw_ref, w_ref, o_ref, h_ref):
    i = pl.program_id(0)
    j = pl.program_id(1)

    def norm_into_scratch(x_ref):
        x = x_ref[...]
        ms = jnp.mean(x * x, axis=-1, keepdims=True)
        h_ref[...] = (x * lax.rsqrt(ms + EPS) * nw_ref[...]).astype(BF16)

    @pl.when((j == 0) & (i < nbp))
    def _():
        norm_into_scratch(xp_ref)

    @pl.when((j == 0) & (i >= nbp))
    def _():
        norm_into_scratch(xs_ref)

    acc = jnp.dot(h_ref[...], w_ref[...], preferred_element_type=F32)
    for c in range(o_ref.shape[0]):
        o_ref[c] = acc[:, c * LANE:(c + 1) * LANE].astype(BF16)


def _inproj(xp, xs, norm_w, w_bf16):
    tp, ts = xp.shape[0], xs.shape[0]
    tm = _largest_divisor(np.gcd(tp, ts), 1024)
    tn = 512
    nbp, nbs = tp // tm, ts // tm
    n_out = w_bf16.shape[1]
    return pl.pallas_call(
        functools.partial(_inproj_kernel, nbp),
        grid=(nbp + nbs, n_out // tn),
        in_specs=[
            pl.BlockSpec((tm, D_MODEL), lambda i, j: (jnp.minimum(i, nbp - 1), 0)),
            pl.BlockSpec((tm, D_MODEL), lambda i, j: (jnp.maximum(i - nbp, 0), 0)),
            pl.BlockSpec((1, D_MODEL), lambda i, j: (0, 0)),
            pl.BlockSpec((D_MODEL, tn), lambda i, j: (0, j)),
        ],
        out_specs=pl.BlockSpec((tn // LANE, tm, LANE), lambda i, j: (j, i, 0)),
        out_shape=jax.ShapeDtypeStruct((n_out // LANE, tp + ts, LANE), BF16),
        scratch_shapes=[pltpu.VMEM((tm, D_MODEL), BF16)],
        compiler_params=_cparams("parallel", "arbitrary"),
        name="inproj",
    )(xp, xs, norm_w, w_bf16)


def _attn_kernel(q_ref, kp_ref, kc_ref, kn_ref, vp_ref, vc_ref, vn_ref, sink_ref, o_ref):
    n = pl.program_id(1)
    nb = pl.num_programs(1)
    blk = ATTN_BLOCK
    qi = lax.broadcasted_iota(jnp.int32, (blk, 3 * blk), 0)
    kc = lax.broadcasted_iota(jnp.int32, (blk, 3 * blk), 1)
    dist = jnp.abs(kc - blk - qi)
    ok = (dist <= WINDOW) & ((kc >= blk) | (n > 0)) & ((kc < 2 * blk) | (n < nb - 1))
    adist = dist.astype(F32)
    scale = HEAD_DIM ** -0.5
    for g in range(ATTN_KV_HEADS):
        kcat = jnp.concatenate([kp_ref[g], kc_ref[g], kn_ref[g]], axis=0)
        vcat = jnp.concatenate([vp_ref[g], vc_ref[g], vn_ref[g]], axis=0)
        for hh in range(ATTN_GROUP):
            h = g * ATTN_GROUP + hh
            slope = 2.0 ** (-8.0 * (h + 1) / ATTN_HEADS)
            s = lax.dot_general(q_ref[h], kcat, (((1,), (1,)), ((), ())), preferred_element_type=F32)
            s = jnp.where(ok, s * scale - slope * adist, -jnp.inf)
            sink = sink_ref[0, h]
            m = jnp.maximum(jnp.max(s, axis=-1, keepdims=True), sink)
            p = jnp.exp(s - m)
            denom = jnp.sum(p, axis=-1, keepdims=True) + jnp.exp(sink - m)
            o = jnp.dot(p.astype(BF16), vcat, preferred_element_type=F32) / denom
            o_ref[:, h * HEAD_DIM:(h + 1) * HEAD_DIM] = o.astype(BF16)


def _attention(proj, sink, n_seq, seq):
    nb = seq // ATTN_BLOCK
    t = n_seq * seq

    def kv_spec(slab_pair, shift):
        def imap(b, n):
            return (slab_pair, b * nb + jnp.clip(n + shift, 0, nb - 1), 0)
        return pl.BlockSpec((ATTN_KV_HEADS, ATTN_BLOCK, LANE), imap)

    return pl.pallas_call(
        _attn_kernel,
        grid=(n_seq, nb),
        in_specs=[
            pl.BlockSpec((ATTN_HEADS, ATTN_BLOCK, LANE), lambda b, n: (SLAB_QA // ATTN_HEADS, b * nb + n, 0)),
            kv_spec(SLAB_KA // ATTN_KV_HEADS, -1), kv_spec(SLAB_KA // ATTN_KV_HEADS, 0),
            kv_spec(SLAB_KA // ATTN_KV_HEADS, 1),
            kv_spec(SLAB_VA // ATTN_KV_HEADS, -1), kv_spec(SLAB_VA // ATTN_KV_HEADS, 0),
            kv_spec(SLAB_VA // ATTN_KV_HEADS, 1),
            pl.BlockSpec(memory_space=pltpu.SMEM),
        ],
        out_specs=pl.BlockSpec((ATTN_BLOCK, ATTN_HEADS * HEAD_DIM), lambda b, n: (b * nb + n, 0)),
        out_shape=jax.ShapeDtypeStruct((t, ATTN_HEADS * HEAD_DIM), BF16),
        compiler_params=_cparams("parallel", "arbitrary"),
        name="attn",
    )(proj, proj, proj, proj, proj, proj, proj, sink)


def _hgrn_constants(c, reverse):
    halves = []
    h = c // 2
    while h >= 1:
        halves.append(h)
        h //= 2
    m = np.zeros((len(halves) + 2, c, c), np.float32)
    level = -np.ones((c, c), np.int32)
    for t in range(c):
        m[0, t, :t + 1] = 1
        m[1, t, t + 1:] = 1
        level[t, t] = 0
    for li, h in enumerate(halves):
        for t in range(c):
            pos = t % (2 * h)
            mid = t - pos + h
            if pos >= h:
                m[2 + li, t, mid:t + 1] = 1
                level[t, t - pos:mid] = li + 1
            else:
                m[2 + li, t, t + 1:mid] = 1
    if reverse:
        m = m[:, ::-1, ::-1]
        level = level[::-1, ::-1]
    m = m.reshape(-1, c)
    return np.ascontiguousarray(np.concatenate([m, m], axis=1)), np.ascontiguousarray(level), len(halves)


def _sigmoid(x):
    return 1.0 / (1.0 + jnp.exp(-x))


def _hgrn_kernel(c, n_levels, q_ref, ff_ref, fb_ref, i_ref, g_ref, lb_ref, nw_ref,
                 mf_ref, mb_ref, lvf_ref, lvb_ref, o_ref, of_ref, e_ref):
    seq = q_ref.shape[1]
    nc = seq // c

    lbr = lb_ref[...]
    lbe = jnp.exp(lbr - jnp.max(lbr, axis=0, keepdims=True))
    lb_all = lbe[0] / jnp.sum(lbe, axis=0)

    def chunk(r0, st, z_ref, lb, m_ref, lv, last_row):
        z = z_ref[0, pl.ds(r0, c), :].astype(F32)
        qp = q_ref[0, pl.ds(r0, c), :].astype(F32)
        qs = qp * _sigmoid(qp)
        v = i_ref[0, pl.ds(r0, c), :]
        sig = _sigmoid(z)
        f = lb + (1.0 - lb) * sig
        k = (1.0 - lb) * (1.0 - sig)
        g = jnp.log(f)
        g_hi = g.astype(BF16)
        g_lo = (g - g_hi.astype(F32)).astype(BF16)
        e_ref[...] = jnp.dot(m_ref[...], jnp.concatenate([g_hi, g_lo], axis=0), preferred_element_type=F32)
        b = e_ref[0:c, :]
        a = jnp.zeros((c, c), F32)
        for lvl in range(n_levels + 1):
            if lvl == 0:
                qq, kk = qs, k
            else:
                w = jnp.exp(e_ref[(1 + lvl) * c:(2 + lvl) * c, :])
                qq, kk = qs * w, k * w
            p = lax.dot_general(qq.astype(BF16), kk.astype(BF16), (((1,), (1,)), ((), ())),
                                preferred_element_type=F32)
            a = jnp.where(lv == lvl, p, a)
        o = jnp.dot(a.astype(BF16), v, preferred_element_type=F32)
        o = o + lax.dot_general((qs * jnp.exp(b)).astype(BF16), st.astype(BF16), (((1,), (1,)), ((), ())),
                                preferred_element_type=F32)
        k_out = (k * jnp.exp(e_ref[c:2 * c, :])).astype(BF16)
        st = st * jnp.exp(b[last_row:last_row + 1, :]) + lax.dot_general(
            v, k_out, (((0,), (0,)), ((), ())), preferred_element_type=F32)
        return o, st

    lvf = lvf_ref[...]
    lvb = lvb_ref[...]
    st0 = jnp.zeros((LANE, LANE), F32)

    def fwd_body(it, st):
        r0 = pl.multiple_of(it * c, c)
        o, st = chunk(r0, st, ff_ref, lb_all[0:1, :], mf_ref, lvf, c - 1)
        of_ref[pl.ds(r0, c), :] = o
        return st

    lax.fori_loop(0, nc, fwd_body, st0)

    def bwd_body(it, st):
        r0 = pl.multiple_of((nc - 1 - it) * c, c)
        o, st = chunk(r0, st, fb_ref, lb_all[1:2, :], mb_ref, lvb, 0)
        tot = of_ref[pl.ds(r0, c), :] + o
        y = tot * lax.rsqrt(jnp.mean(tot * tot, axis=-1, keepdims=True) + EPS) * nw_ref[...]
        gp = g_ref[0, pl.ds(r0, c), :].astype(F32)
        o_ref[pl.ds(r0, c), :] = (y * (gp * _sigmoid(gp))).astype(BF16)
        return st

    lax.fori_loop(0, nc, bwd_body, st0)


def _hgrn(proj, hg_lb, norm_w, n_seq, seq):
    c = HG_CHUNK
    mf, lvf, n_levels = _hgrn_constants(c, False)
    mb, lvb, _ = _hgrn_constants(c, True)
    t = n_seq * seq

    def slab(first):
        return pl.BlockSpec((1, seq, LANE), lambda b, h: (first + h, b, 0))

    def const(arr):
        return pl.BlockSpec(arr.shape, lambda b, h: (0,) * arr.ndim)

    return pl.pallas_call(
        functools.partial(_hgrn_kernel, c, n_levels),
        grid=(n_seq, HG_HEADS),
        in_specs=[
            slab(SLAB_QH), slab(SLAB_FF), slab(SLAB_FB), slab(SLAB_IH), slab(SLAB_GH),
            pl.BlockSpec((hg_lb.shape[0], 2, LANE), lambda b, h: (0, 0, h)),
            pl.BlockSpec((1, LANE), lambda b, h: (0, 0)),
            const(mf), const(mb), const(lvf), const(lvb),
        ],
        out_specs=pl.BlockSpec((seq, LANE), lambda b, h: (b, h)),
        out_shape=jax.ShapeDtypeStruct((t, HG_HEADS * LANE), BF16),
        scratch_shapes=[pltpu.VMEM((seq, LANE), F32), pltpu.VMEM((mf.shape[0], LANE), F32)],
        compiler_params=_cparams("parallel", "arbitrary"),
        name="hgrn",
    )(proj, proj, proj, proj, proj, hg_lb, norm_w,
      jnp.asarray(mf, BF16), jnp.asarray(mb, BF16), jnp.asarray(lvf), jnp.asarray(lvb))


def _outproj_kernel(nbp, xp_ref, xs_ref, a_ref, hg_ref, wo_ref, n2_ref, wrh_ref, wrl_ref, br_ref,
                    x1_ref, h2_ref, ri_ref, rw_ref):
    i = pl.program_id(0)
    half = a_ref.shape[1]
    acc = jnp.dot(a_ref[...], wo_ref[0:half, :], preferred_element_type=F32)
    acc = acc + jnp.dot(hg_ref[...], wo_ref[half:2 * half, :], preferred_element_type=F32)

    @pl.when(i < nbp)
    def _():
        x1_ref[...] = xp_ref[...] + acc

    @pl.when(i >= nbp)
    def _():
        x1_ref[...] = xs_ref[...] + acc

    x1 = x1_ref[...]
    h2 = x1 * lax.rsqrt(jnp.mean(x1 * x1, axis=-1, keepdims=True) + EPS) * n2_ref[...]
    h2_ref[...] = h2

    h_hi = h2.astype(BF16)
    h_lo = (h2 - h_hi.astype(F32)).astype(BF16)
    logits = (jnp.dot(h_hi, wrh_ref[...], preferred_element_type=F32)
              + jnp.dot(h_lo, wrh_ref[...], preferred_element_type=F32)
              + jnp.dot(h_hi, wrl_ref[...], preferred_element_type=F32)) + br_ref[...]

    lane = lax.broadcasted_iota(jnp.int32, logits.shape, 1).astype(F32)
    far = float(LANE)
    is_g = lane < N_GROUPS
    gl = jnp.where(is_g, logits, -jnp.inf)
    gmax = jnp.max(gl, axis=-1, keepdims=True)
    gidx = jnp.min(jnp.where(gl == gmax, lane, far), axis=-1, keepdims=True)
    g_w = 1.0 / jnp.sum(jnp.where(is_g, jnp.exp(logits - gmax), 0.0), axis=-1, keepdims=True)
    lo = N_GROUPS + gidx * EXPERTS_PER_GROUP
    in_grp = (lane >= lo) & (lane < lo + EXPERTS_PER_GROUP)
    el = jnp.where(in_grp, logits, -jnp.inf)
    m1 = jnp.max(el, axis=-1, keepdims=True)
    i1 = jnp.min(jnp.where(el == m1, lane, far), axis=-1, keepdims=True)
    el2 = jnp.where(lane == i1, -jnp.inf, el)
    m2 = jnp.max(el2, axis=-1, keepdims=True)
    i2 = jnp.min(jnp.where(el2 == m2, lane, far), axis=-1, keepdims=True)
    p2 = jnp.exp(m2 - m1)
    w1 = 1.0 / (1.0 + p2)
    w2 = p2 / (1.0 + p2)
    ri_ref[...] = jnp.where(lane == 0.0, i1 - N_GROUPS, jnp.where(lane == 1.0, i2 - N_GROUPS, 0.0)).astype(jnp.int32)
    rw_ref[...] = jnp.where(lane == 0.0, g_w * w1, jnp.where(lane == 1.0, g_w * w2, 0.0))


def _outproj(xp, xs, attn, hg, wo_bf16, norm2_w, wr_hi, wr_lo, br):
    tp, ts = xp.shape[0], xs.shape[0]
    t = tp + ts
    tm = _largest_divisor(np.gcd(tp, ts), 256)
    nbp, nbs = tp // tm, ts // tm
    half = attn.shape[1]

    def const(shape):
        return pl.BlockSpec(shape, lambda i: (0, 0))

    def rows(width):
        return pl.BlockSpec((tm, width), lambda i: (i, 0))

    return pl.pallas_call(
        functools.partial(_outproj_kernel, nbp),
        grid=(nbp + nbs,),
        in_specs=[
            pl.BlockSpec((tm, D_MODEL), lambda i: (jnp.minimum(i, nbp - 1), 0)),
            pl.BlockSpec((tm, D_MODEL), lambda i: (jnp.maximum(i - nbp, 0), 0)),
            rows(half), rows(half),
            const((2 * half, D_MODEL)), const((1, D_MODEL)),
            const((D_MODEL, LANE)), const((D_MODEL, LANE)), const((1, LANE)),
        ],
        out_specs=[rows(D_MODEL), rows(D_MODEL), rows(LANE), rows(LANE)],
        out_shape=[
            jax.ShapeDtypeStruct((t, D_MODEL), F32),
            jax.ShapeDtypeStruct((t, D_MODEL), F32),
            jax.ShapeDtypeStruct((t, LANE), jnp.int32),
            jax.ShapeDtypeStruct((t, LANE), F32),
        ],
        compiler_params=_cparams("parallel"),
        name="outproj",
    )(xp, xs, attn, hg, wo_bf16, norm2_w, wr_hi, wr_lo, br)


def _plan_expert_tiles(e1, e2, tm):
    t = e1.shape[0]
    n_slots = 2 * t
    n_tiles = n_slots // tm + N_EXPERTS
    p = n_tiles * tm
    ef = jnp.concatenate([e1, e2])
    onehot = (ef[:, None] == jnp.arange(N_EXPERTS, dtype=jnp.int32)[None, :]).astype(jnp.int32)
    csum = jnp.cumsum(onehot, axis=0)
    counts = csum[-1]
    rank = jnp.sum((csum - onehot) * onehot, axis=1)
    tiles_per = (counts + tm - 1) // tm
    tile_end = jnp.cumsum(tiles_per)
    seg_start = (tile_end - tiles_per) * tm
    ppos = seg_start[ef] + rank
    slot = jnp.arange(n_slots, dtype=jnp.int32)
    src_tok = jnp.zeros((p,), jnp.int32).at[ppos].set(slot % t)
    real_dest = jnp.full((p,), -1, jnp.int32).at[ppos].set(slot)
    is_pad = real_dest < 0
    pad_rank = jnp.cumsum(is_pad.astype(jnp.int32)) - 1
    dest = jnp.where(is_pad, n_slots + pad_rank, real_dest)
    tile_expert = jnp.minimum(
        jnp.searchsorted(tile_end, jnp.arange(n_tiles, dtype=jnp.int32), side="right"), N_EXPERTS - 1
    ).astype(jnp.int32)
    return src_tok.reshape(n_tiles, 1, tm), dest.reshape(n_tiles, 1, tm), tile_expert, n_tiles


def _experts_kernel(te_ref, src_ref, srcn_ref, dst_ref, h2_hbm, wg_ref, wu_ref, wd_ref, y_hbm,
                    gbuf, obuf, wg_bf, wu_bf, wd_bf, gsem, ssem):
    i = pl.program_id(0)
    n = pl.num_programs(0)
    tm = gbuf.shape[1]
    slot = i % 2

    def gather_copy(tok, k, s):
        return pltpu.make_async_copy(h2_hbm.at[pl.ds(tok, 1), :], gbuf.at[s, pl.ds(k, 1), :], gsem.at[s, k])

    def scatter_copy(row, k, s):
        return pltpu.make_async_copy(obuf.at[s, pl.ds(k, 1), :], y_hbm.at[pl.ds(row, 1), :], ssem.at[s, k])

    def start_gather(idx_ref, s):
        def body(k, carry):
            gather_copy(idx_ref[0, 0, k], k, s).start()
            return carry
        lax.fori_loop(0, tm, body, 0)

    def wait_rows(make, s):
        def body(k, carry):
            make(0, k, s).wait()
            return carry
        lax.fori_loop(0, tm, body, 0)

    @pl.when(i == 0)
    def _():
        start_gather(src_ref, 0)

    @pl.when(i + 1 < n)
    def _():
        start_gather(srcn_ref, 1 - slot)

    changed = (i == 0) | (te_ref[i] != te_ref[jnp.maximum(i - 1, 0)])

    @pl.when(changed)
    def _():
        wg_bf[...] = wg_ref[0].astype(BF16)
        wu_bf[...] = wu_ref[0].astype(BF16)
        wd_bf[...] = wd_ref[0].astype(BF16)

    wait_rows(gather_copy, slot)
    hb = gbuf[slot].astype(BF16)
    gate = jnp.dot(hb, wg_bf[...], preferred_element_type=F32)
    up = jnp.dot(hb, wu_bf[...], preferred_element_type=F32)
    act = (gate * _sigmoid(gate) * up).astype(BF16)
    y = jnp.dot(act, wd_bf[...], preferred_element_type=F32)

    @pl.when(i >= 2)
    def _():
        wait_rows(scatter_copy, slot)

    obuf[slot] = y

    def sbody(k, carry):
        scatter_copy(dst_ref[0, 0, k], k, slot).start()
        return carry
    lax.fori_loop(0, tm, sbody, 0)

    @pl.when(i == n - 1)
    def _():
        wait_rows(scatter_copy, slot)

        @pl.when(n >= 2)
        def _():
            wait_rows(scatter_copy, 1 - slot)


def _experts(h2, src_tok, dest, tile_expert, n_tiles, w_gate, w_up, w_down):
    tm = EXPERT_TILE
    p = n_tiles * tm

    def idx_spec(shift):
        return pl.BlockSpec((1, 1, tm), lambda i, te: (jnp.minimum(i + shift, n_tiles - 1), 0, 0),
                            memory_space=pltpu.SMEM)

    def w_spec(shape):
        return pl.BlockSpec((1,) + shape, lambda i, te: (te[i], 0, 0))

    grid_spec = pltpu.PrefetchScalarGridSpec(
        num_scalar_prefetch=1,
        grid=(n_tiles,),
        in_specs=[
            idx_spec(0), idx_spec(1), idx_spec(0),
            pl.BlockSpec(memory_space=pl.ANY),
            w_spec((D_MODEL, D_EXPERT)), w_spec((D_MODEL, D_EXPERT)), w_spec((D_EXPERT, D_MODEL)),
        ],
        out_specs=pl.BlockSpec(memory_space=pl.ANY),
        scratch_shapes=[
            pltpu.VMEM((2, tm, D_MODEL), F32),
            pltpu.VMEM((2, tm, D_MODEL), F32),
            pltpu.VMEM((D_MODEL, D_EXPERT), BF16),
            pltpu.VMEM((D_MODEL, D_EXPERT), BF16),
            pltpu.VMEM((D_EXPERT, D_MODEL), BF16),
            pltpu.SemaphoreType.DMA((2, tm)),
            pltpu.SemaphoreType.DMA((2, tm)),
        ],
    )
    return pl.pallas_call(
        _experts_kernel,
        grid_spec=grid_spec,
        out_shape=jax.ShapeDtypeStruct((p, D_MODEL), F32),
        compiler_params=_cparams("arbitrary"),
        name="experts",
    )(tile_expert, src_tok, src_tok, dest, h2, w_gate, w_up, w_down)


def _final_kernel(x1_ref, ya_ref, yb_ref, rw_ref, nw_ref, o_ref):
    rw = rw_ref[...]
    x = x1_ref[...] + rw[:, 0:1] * ya_ref[...] + rw[:, 1:2] * yb_ref[...]
    o_ref[...] = x * lax.rsqrt(jnp.mean(x * x, axis=-1, keepdims=True) + EPS) * nw_ref[...]


def _final(x1, y, rw, norm_w, row0, n_rows):
    t = x1.shape[0]
    tm = _largest_divisor(np.gcd(np.gcd(row0, n_rows), t), 512)
    b0 = row0 // tm
    bt = t // tm

    return pl.pallas_call(
        _final_kernel,
        grid=(n_rows // tm,),
        in_specs=[
            pl.BlockSpec((tm, D_MODEL), lambda i: (b0 + i, 0)),
            pl.BlockSpec((tm, D_MODEL), lambda i: (b0 + i, 0)),
            pl.BlockSpec((tm, D_MODEL), lambda i: (bt + b0 + i, 0)),
            pl.BlockSpec((tm, LANE), lambda i: (b0 + i, 0)),
            pl.BlockSpec((1, D_MODEL), lambda i: (0, 0)),
        ],
        out_specs=pl.BlockSpec((tm, D_MODEL), lambda i: (i, 0)),
        out_shape=jax.ShapeDtypeStruct((n_rows, D_MODEL), F32),
        compiler_params=_cparams("parallel"),
        name="final",
    )(x1, y, y, rw, norm_w)


def kernel(x_prompt, x_sample, norm1_w, w_in, attn_sink, hg_lb, hg_norm_w, w_out, norm2_w, w_router_g,
           b_router_g, w_router_e, b_router_e, w_gate, w_up, w_down, final_norm_w):
    bp, seq, d = x_prompt.shape
    bs, seq_s, _ = x_sample.shape
    assert d == D_MODEL and seq == seq_s and seq % max(ATTN_BLOCK, HG_CHUNK) == 0
    assert w_in.shape[0] == 1 and w_in.shape[2] == N_SLABS * LANE
    n_seq = bp + bs
    tp, ts = bp * seq, bs * seq
    xp = x_prompt.reshape(tp, d)
    xs = x_sample.reshape(ts, d)

    proj = _inproj(xp, xs, norm1_w, w_in[0].astype(BF16))
    attn = _attention(proj, attn_sink, n_seq, seq)
    hg = _hgrn(proj, hg_lb, hg_norm_w, n_seq, seq)

    wr = jnp.concatenate([w_router_g[0], w_router_e[0].reshape(d, N_EXPERTS)], axis=1)
    wr = jnp.pad(wr, ((0, 0), (0, LANE - wr.shape[1])))
    wr_hi = wr.astype(BF16)
    wr_lo = (wr - wr_hi.astype(F32)).astype(BF16)
    br = jnp.concatenate([b_router_g[0], b_router_e[0].reshape(N_EXPERTS)])
    br = jnp.pad(br, (0, LANE - br.shape[0])).reshape(1, LANE)
    x1, h2, ri, rw = _outproj(xp, xs, attn, hg, w_out[0].astype(BF16), norm2_w, wr_hi, wr_lo, br)

    src_tok, dest, tile_expert, n_tiles = _plan_expert_tiles(ri[:, 0], ri[:, 1], EXPERT_TILE)
    y = _experts(h2, src_tok, dest, tile_expert, n_tiles, w_gate[0], w_up[0], w_down[0])

    fnw = final_norm_w.reshape(1, d)
    y_prompt = _final(x1, y, rw, fnw, 0, tp).reshape(bp, seq, d)
    y_sample = _final(x1, y, rw, fnw, tp, ts).reshape(bs, seq, d)
    return (y_prompt, y_sample)
```

```python
import functools

import numpy as np
import jax
import jax.numpy as jnp
from jax import lax
from jax.experimental import pallas as pl
from jax.experimental.pallas import tpu as pltpu

F32 = jnp.float32
BF16 = jnp.bfloat16

LANE = 128
D_MODEL = 2048
ATTN_HEADS = 8
ATTN_KV_HEADS = 2
ATTN_GROUP = ATTN_HEADS // ATTN_KV_HEADS
HEAD_DIM = LANE
WINDOW = 128
ATTN_BLOCK = 128
HG_HEADS = 8
HG_CHUNK = 128
HG_VPU_MIN_HALF = 4
N_GROUPS = 4
EXPERTS_PER_GROUP = 8
N_EXPERTS = N_GROUPS * EXPERTS_PER_GROUP
D_EXPERT = 512
EPS = 1e-6

SLAB_QA, SLAB_KA, SLAB_VA, SLAB_QH, SLAB_FF, SLAB_FB, SLAB_IH, SLAB_GH = 0, 8, 10, 12, 20, 28, 36, 44
N_SLABS = 52

EXPERT_TILE = 256
ROUTE_TILE = 256
VMEM_LIMIT = 56 * 1024 * 1024


def _largest_divisor(n, cap):
    n = int(n)
    d = min(n, cap)
    while n % d:
        d -= 1
    return d


def _cparams(*sem):
    return pltpu.CompilerParams(dimension_semantics=sem, vmem_limit_bytes=VMEM_LIMIT)


def _inproj_kernel(nbp, xp_ref, xs_ref, nw_ref, w_ref, o_ref, h_ref):
    i = pl.program_id(0)
    j = pl.program_id(1)

    def norm_into_scratch(x_ref):
        x = x_ref[...]
        ms = jnp.mean(x * x, axis=-1, keepdims=True)
        h_ref[...] = (x * lax.rsqrt(ms + EPS) * nw_ref[...]).astype(BF16)

    @pl.when((j == 0) & (i < nbp))
    def _():
        norm_into_scratch(xp_ref)

    @pl.when((j == 0) & (i >= nbp))
    def _():
        norm_into_scratch(xs_ref)

    acc = jnp.dot(h_ref[...], w_ref[...], preferred_element_type=F32)
    for c in range(o_ref.shape[0]):
        o_ref[c] = acc[:, c * LANE:(c + 1) * LANE].astype(BF16)


def _inproj(xp, xs, norm_w, w_bf16):
    tp, ts = xp.shape[0], xs.shape[0]
    tm = _largest_divisor(np.gcd(tp, ts), 1024)
    tn = 512
    nbp, nbs = tp // tm, ts // tm
    n_out = w_bf16.shape[1]
    return pl.pallas_call(
        functools.partial(_inproj_kernel, nbp),
        grid=(nbp + nbs, n_out // tn),
        in_specs=[
            pl.BlockSpec((tm, D_MODEL), lambda i, j: (jnp.minimum(i, nbp - 1), 0)),
            pl.BlockSpec((tm, D_MODEL), lambda i, j: (jnp.maximum(i - nbp, 0), 0)),
            pl.BlockSpec((1, D_MODEL), lambda i, j: (0, 0)),
            pl.BlockSpec((D_MODEL, tn), lambda i, j: (0, j)),
        ],
        out_specs=pl.BlockSpec((tn // LANE, tm, LANE), lambda i, j: (j, i, 0)),
        out_shape=jax.ShapeDtypeStruct((n_out // LANE, tp + ts, LANE), BF16),
        scratch_shapes=[pltpu.VMEM((tm, D_MODEL), BF16)],
        compiler_params=_cparams("parallel", "arbitrary"),
        name="inproj",
    )(xp, xs, norm_w, w_bf16)


def _attn_kernel(q_ref, kp_ref, kc_ref, kn_ref, vp_ref, vc_ref, vn_ref, sink_ref, o_ref):
    n = pl.program_id(1)
    nb = pl.num_programs(1)
    blk = ATTN_BLOCK
    qi = lax.broadcasted_iota(jnp.int32, (blk, 3 * blk), 0)
    kc = lax.broadcasted_iota(jnp.int32, (blk, 3 * blk), 1)
    dist = jnp.abs(kc - blk - qi)
    ok = (dist <= WINDOW) & ((kc >= blk) | (n > 0)) & ((kc < 2 * blk) | (n < nb - 1))
    adist = dist.astype(F32)
    scale = HEAD_DIM ** -0.5
    for g in range(ATTN_KV_HEADS):
        kcat = jnp.concatenate([kp_ref[g], kc_ref[g], kn_ref[g]], axis=0)
        vcat = jnp.concatenate([vp_ref[g], vc_ref[g], vn_ref[g]], axis=0)
        for hh in range(ATTN_GROUP):
            h = g * ATTN_GROUP + hh
            slope = 2.0 ** (-8.0 * (h + 1) / ATTN_HEADS)
            s = lax.dot_general(q_ref[h], kcat, (((1,), (1,)), ((), ())), preferred_element_type=F32)
            s = jnp.where(ok, s * scale - slope * adist, -jnp.inf)
            sink = sink_ref[0, h]
            m = jnp.maximum(jnp.max(s, axis=-1, keepdims=True), sink)
            p = jnp.exp(s - m)
            denom = jnp.sum(p, axis=-1, keepdims=True) + jnp.exp(sink - m)
            o = jnp.dot(p.astype(BF16), vcat, preferred_element_type=F32) / denom
            o_ref[:, h * HEAD_DIM:(h + 1) * HEAD_DIM] = o.astype(BF16)


def _attention(proj, sink, n_seq, seq):
    nb = seq // ATTN_BLOCK
    t = n_seq * seq

    def kv_spec(slab_pair, shift):
        def imap(b, n):
            return (slab_pair, b * nb + jnp.clip(n + shift, 0, nb - 1), 0)
        return pl.BlockSpec((ATTN_KV_HEADS, ATTN_BLOCK, LANE), imap)

    return pl.pallas_call(
        _attn_kernel,
        grid=(n_seq, nb),
        in_specs=[
            pl.BlockSpec((ATTN_HEADS, ATTN_BLOCK, LANE), lambda b, n: (SLAB_QA // ATTN_HEADS, b * nb + n, 0)),
            kv_spec(SLAB_KA // ATTN_KV_HEADS, -1), kv_spec(SLAB_KA // ATTN_KV_HEADS, 0),
            kv_spec(SLAB_KA // ATTN_KV_HEADS, 1),
            kv_spec(SLAB_VA // ATTN_KV_HEADS, -1), kv_spec(SLAB_VA // ATTN_KV_HEADS, 0),
            kv_spec(SLAB_VA // ATTN_KV_HEADS, 1),
            pl.BlockSpec(memory_space=pltpu.SMEM),
        ],
        out_specs=pl.BlockSpec((ATTN_BLOCK, ATTN_HEADS * HEAD_DIM), lambda b, n: (b * nb + n, 0)),
        out_shape=jax.ShapeDtypeStruct((t, ATTN_HEADS * HEAD_DIM), BF16),
        compiler_params=_cparams("parallel", "arbitrary"),
        name="attn",
    )(proj, proj, proj, proj, proj, proj, proj, sink)


def _hgrn_halves(c):
    halves = []
    h = c // 2
    while h >= 1:
        halves.append(h)
        h //= 2
    return halves


def _hgrn_constants(c, reverse):
    halves = _hgrn_halves(c)
    m = np.zeros((2, c, c), np.float32)
    level = -np.ones((c, c), np.int32)
    sign = np.zeros((sum(h >= HG_VPU_MIN_HALF for h in halves), c, LANE), np.float32)
    for t in range(c):
        m[0, t, :t + 1] = 1
        level[t, t] = 0
    for li, h in enumerate(halves):
        for t in range(c):
            pos = t % (2 * h)
            mid = t - pos + h
            if pos >= h:
                level[t, t - pos:mid] = li + 1
            if h == 2:
                if pos >= h:
                    m[1, t, mid:t + 1] = 1
                else:
                    m[1, t, t + 1:mid] = 1
            if h >= HG_VPU_MIN_HALF:
                sign[li, t, :] = 1.0 if pos >= h else -1.0
    if reverse:
        m = m[:, ::-1, ::-1]
        level = level[::-1, ::-1]
        sign = sign[:, ::-1, :]
    m = m.reshape(-1, c)
    return (np.ascontiguousarray(np.concatenate([m, m], axis=1)), np.ascontiguousarray(level),
            np.ascontiguousarray(sign))


def _sigmoid(x):
    return 1.0 / (1.0 + jnp.exp(-x))


def _hgrn_kernel(c, q_ref, ff_ref, fb_ref, i_ref, g_ref, lb_ref, nw_ref,
                 mf_ref, mb_ref, lvf_ref, lvb_ref, sgf_ref, sgb_ref, o_ref, of_ref, ob_ref):
    seq = q_ref.shape[1]
    nc = seq // c
    halves = _hgrn_halves(c)

    lbr = lb_ref[...]
    lbe = jnp.exp(lbr - jnp.max(lbr, axis=0, keepdims=True))
    lb_all = lbe[0] / jnp.sum(lbe, axis=0)
    odd_row = (lax.broadcasted_iota(jnp.int32, (c, LANE), 0) & 1) == 1

    def chunk(r0, st, z_ref, lb, m_ref, lv_ref, sg_ref, reverse):
        z = z_ref[0, pl.ds(r0, c), :].astype(F32)
        qp = q_ref[0, pl.ds(r0, c), :].astype(F32)
        qs = qp * _sigmoid(qp)
        v = i_ref[0, pl.ds(r0, c), :]
        sig = _sigmoid(z)
        f = lb + (1.0 - lb) * sig
        k = (1.0 - lb) * (1.0 - sig)
        g = jnp.log(f)
        g_hi = g.astype(BF16)
        g_lo = (g - g_hi.astype(F32)).astype(BF16)
        e = jnp.dot(m_ref[...], jnp.concatenate([g_hi, g_lo], axis=0), preferred_element_type=F32)
        b = e[0:c, :]
        lv = lv_ref[...]
        p = lax.dot_general(qs.astype(BF16), k.astype(BF16), (((1,), (1,)), ((), ())),
                            preferred_element_type=F32)
        a = jnp.where(lv == 0, p, 0.0)
        for li, h in enumerate(halves):
            if h == 1:
                w = jnp.where(odd_row, 1.0, f) if reverse else jnp.where(odd_row, f, 1.0)
            elif h == 2:
                w = jnp.exp(e[c:2 * c, :])
            else:
                b3 = b.reshape(c // (2 * h), 2 * h, LANE)
                mid = h if reverse else h - 1
                d = (b3 - b3[:, mid:mid + 1, :]).reshape(c, LANE)
                w = jnp.exp(d * sg_ref[li])
            p = lax.dot_general((qs * w).astype(BF16), (k * w).astype(BF16), (((1,), (1,)), ((), ())),
                                preferred_element_type=F32)
            a = jnp.where(lv == li + 1, p, a)
        last = 0 if reverse else c - 1
        b_last = b[last:last + 1, :]
        o = jnp.dot(a.astype(BF16), v, preferred_element_type=F32)
        o = o + lax.dot_general((qs * jnp.exp(b)).astype(BF16), st.astype(BF16), (((1,), (1,)), ((), ())),
                                preferred_element_type=F32)
        k_out = (k * jnp.exp(b_last - b)).astype(BF16)
        st = st * jnp.exp(b_last) + lax.dot_general(
            v, k_out, (((0,), (0,)), ((), ())), preferred_element_type=F32)
        return o, st

    st0 = jnp.zeros((LANE, LANE), F32)

    def scan_body(it, carry):
        st_f, st_b = carry
        rf = pl.multiple_of(it * c, c)
        rb = pl.multiple_of((nc - 1 - it) * c, c)
        o_f, st_f = chunk(rf, st_f, ff_ref, lb_all[0:1, :], mf_ref, lvf_ref, sgf_ref, False)
        o_b, st_b = chunk(rb, st_b, fb_ref, lb_all[1:2, :], mb_ref, lvb_ref, sgb_ref, True)
        of_ref[pl.ds(rf, c), :] = o_f
        ob_ref[pl.ds(rb, c), :] = o_b
        return st_f, st_b

    lax.fori_loop(0, nc, scan_body, (st0, st0))

    def out_body(it, carry):
        r0 = pl.multiple_of(it * c, c)
        tot = of_ref[pl.ds(r0, c), :] + ob_ref[pl.ds(r0, c), :]
        y = tot * lax.rsqrt(jnp.mean(tot * tot, axis=-1, keepdims=True) + EPS) * nw_ref[...]
        gp = g_ref[0, pl.ds(r0, c), :].astype(F32)
        o_ref[pl.ds(r0, c), :] = (y * (gp * _sigmoid(gp))).astype(BF16)
        return carry

    lax.fori_loop(0, nc, out_body, 0)


def _hgrn(proj, hg_lb, norm_w, n_seq, seq):
    c = HG_CHUNK
    mf, lvf, sgf = _hgrn_constants(c, False)
    mb, lvb, sgb = _hgrn_constants(c, True)
    t = n_seq * seq

    def slab(first):
        return pl.BlockSpec((1, seq, LANE), lambda b, h: (first + h, b, 0))

    def const(arr):
        return pl.BlockSpec(arr.shape, lambda b, h: (0,) * arr.ndim)

    return pl.pallas_call(
        functools.partial(_hgrn_kernel, c),
        grid=(n_seq, HG_HEADS),
        in_specs=[
            slab(SLAB_QH), slab(SLAB_FF), slab(SLAB_FB), slab(SLAB_IH), slab(SLAB_GH),
            pl.BlockSpec((hg_lb.shape[0], 2, LANE), lambda b, h: (0, 0, h)),
            pl.BlockSpec((1, LANE), lambda b, h: (0, 0)),
            const(mf), const(mb), const(lvf), const(lvb), const(sgf), const(sgb),
        ],
        out_specs=pl.BlockSpec((seq, LANE), lambda b, h: (b, h)),
        out_shape=jax.ShapeDtypeStruct((t, HG_HEADS * LANE), BF16),
        scratch_shapes=[pltpu.VMEM((seq, LANE), F32), pltpu.VMEM((seq, LANE), F32)],
        compiler_params=_cparams("parallel", "arbitrary"),
        name="hgrn",
    )(proj, proj, proj, proj, proj, hg_lb, norm_w,
      jnp.asarray(mf, BF16), jnp.asarray(mb, BF16), jnp.asarray(lvf), jnp.asarray(lvb),
      jnp.asarray(sgf), jnp.asarray(sgb))


def _outproj_kernel(nbp, xp_ref, xs_ref, a_ref, hg_ref, wo_ref, n2_ref, wrh_ref, wrl_ref, br_ref, tri_ref,
                    x1_ref, h2_ref, ri_ref, rw_ref, cnt_ref, run_ref):
    i = pl.program_id(0)
    half = a_ref.shape[1]
    acc = jnp.dot(a_ref[...], wo_ref[0:half, :], preferred_element_type=F32)
    acc = acc + jnp.dot(hg_ref[...], wo_ref[half:2 * half, :], preferred_element_type=F32)

    @pl.when(i < nbp)
    def _():
        x1_ref[...] = xp_ref[...] + acc

    @pl.when(i >= nbp)
    def _():
        x1_ref[...] = xs_ref[...] + acc

    @pl.when(i == 0)
    def _():
        run_ref[...] = jnp.zeros_like(run_ref)

    x1 = x1_ref[...]
    h2 = x1 * lax.rsqrt(jnp.mean(x1 * x1, axis=-1, keepdims=True) + EPS) * n2_ref[...]
    h2_ref[...] = h2

    h_hi = h2.astype(BF16)
    h_lo = (h2 - h_hi.astype(F32)).astype(BF16)
    logits = (jnp.dot(h_hi, wrh_ref[...], preferred_element_type=F32)
              + jnp.dot(h_lo, wrh_ref[...], preferred_element_type=F32)
              + jnp.dot(h_hi, wrl_ref[...], preferred_element_type=F32)) + br_ref[...]

    lane = lax.broadcasted_iota(jnp.int32, logits.shape, 1).astype(F32)
    far = float(LANE)
    is_g = lane < N_GROUPS
    gl = jnp.where(is_g, logits, -jnp.inf)
    gmax = jnp.max(gl, axis=-1, keepdims=True)
    gidx = jnp.min(jnp.where(gl == gmax, lane, far), axis=-1, keepdims=True)
    g_w = 1.0 / jnp.sum(jnp.where(is_g, jnp.exp(logits - gmax), 0.0), axis=-1, keepdims=True)
    lo = N_GROUPS + gidx * EXPERTS_PER_GROUP
    in_grp = (lane >= lo) & (lane < lo + EXPERTS_PER_GROUP)
    el = jnp.where(in_grp, logits, -jnp.inf)
    m1 = jnp.max(el, axis=-1, keepdims=True)
    i1 = jnp.min(jnp.where(el == m1, lane, far), axis=-1, keepdims=True)
    el2 = jnp.where(lane == i1, -jnp.inf, el)
    m2 = jnp.max(el2, axis=-1, keepdims=True)
    i2 = jnp.min(jnp.where(el2 == m2, lane, far), axis=-1, keepdims=True)
    p2 = jnp.exp(m2 - m1)
    w1 = 1.0 / (1.0 + p2)
    w2 = p2 / (1.0 + p2)

    oh1 = lane == i1
    oh2 = lane == i2
    one1 = jnp.where(oh1, 1.0, 0.0)
    one2 = jnp.where(oh2, 1.0, 0.0)
    before1 = jnp.dot(tri_ref[...], one1.astype(BF16), preferred_element_type=F32)
    before2 = jnp.dot(tri_ref[...], one2.astype(BF16), preferred_element_type=F32)
    run = run_ref[...]
    tot1 = jnp.sum(one1, axis=0, keepdims=True)
    r1 = jnp.sum(jnp.where(oh1, run + before1, 0.0), axis=-1, keepdims=True)
    r2 = jnp.sum(jnp.where(oh2, run + tot1 + before2, 0.0), axis=-1, keepdims=True)
    run = run + tot1 + jnp.sum(one2, axis=0, keepdims=True)
    run_ref[...] = run
    cnt_ref[...] = run.astype(jnp.int32)

    ri = jnp.where(lane == 0.0, i1 - N_GROUPS, jnp.where(lane == 1.0, i2 - N_GROUPS,
                   jnp.where(lane == 2.0, r1, jnp.where(lane == 3.0, r2, 0.0))))
    ri_ref[...] = ri.astype(jnp.int32)
    rw_ref[...] = jnp.where(lane == 0.0, g_w * w1, jnp.where(lane == 1.0, g_w * w2, 0.0))


def _outproj(xp, xs, attn, hg, wo_bf16, norm2_w, wr_hi, wr_lo, br):
    tp, ts = xp.shape[0], xs.shape[0]
    t = tp + ts
    tm = _largest_divisor(np.gcd(tp, ts), ROUTE_TILE)
    nbp, nbs = tp // tm, ts // tm
    half = attn.shape[1]
    strict_lower = jnp.asarray(np.tril(np.ones((tm, tm), np.float32), -1), BF16)

    def const(shape):
        return pl.BlockSpec(shape, lambda i: (0, 0))

    def rows(width):
        return pl.BlockSpec((tm, width), lambda i: (i, 0))

    return pl.pallas_call(
        functools.partial(_outproj_kernel, nbp),
        grid=(nbp + nbs,),
        in_specs=[
            pl.BlockSpec((tm, D_MODEL), lambda i: (jnp.minimum(i, nbp - 1), 0)),
            pl.BlockSpec((tm, D_MODEL), lambda i: (jnp.maximum(i - nbp, 0), 0)),
            rows(half), rows(half),
            const((2 * half, D_MODEL)), const((1, D_MODEL)),
            const((D_MODEL, LANE)), const((D_MODEL, LANE)), const((1, LANE)), const((tm, tm)),
        ],
        out_specs=[rows(D_MODEL), rows(D_MODEL), rows(LANE), rows(LANE), const((1, LANE))],
        out_shape=[
            jax.ShapeDtypeStruct((t, D_MODEL), F32),
            jax.ShapeDtypeStruct((t, D_MODEL), F32),
            jax.ShapeDtypeStruct((t, LANE), jnp.int32),
            jax.ShapeDtypeStruct((t, LANE), F32),
            jax.ShapeDtypeStruct((1, LANE), jnp.int32),
        ],
        scratch_shapes=[pltpu.VMEM((1, LANE), F32)],
        compiler_params=_cparams("arbitrary"),
        name="outproj",
    )(xp, xs, attn, hg, wo_bf16, norm2_w, wr_hi, wr_lo, br, strict_lower)


def _plan_expert_tiles(ri, counts, tm):
    t = ri.shape[0]
    n_slots = 2 * t
    n_tiles = n_slots // tm + N_EXPERTS
    p = n_tiles * tm
    ef = jnp.concatenate([ri[:, 0], ri[:, 1]])
    rank = jnp.concatenate([ri[:, 2], ri[:, 3]])
    tiles_per = (counts + tm - 1) // tm
    tile_end = jnp.cumsum(tiles_per)
    seg_start = (tile_end - tiles_per) * tm
    experts = jnp.arange(N_EXPERTS, dtype=jnp.int32)
    ppos = jnp.sum(jnp.where(ef[:, None] == experts[None, :], seg_start[None, :], 0), axis=1) + rank
    slot = jnp.arange(n_slots, dtype=jnp.int32)
    real_dest = jnp.full((p,), -1, jnp.int32).at[ppos].set(slot, unique_indices=True)
    is_pad = real_dest < 0
    pos = jnp.arange(p, dtype=jnp.int32)
    dest = jnp.where(is_pad, n_slots + pos % (2 * tm), real_dest)
    src_tok = jnp.where(is_pad, 0, real_dest % t)
    tiles = jnp.arange(n_tiles, dtype=jnp.int32)
    tile_expert = jnp.minimum(jnp.sum((tile_end[None, :] <= tiles[:, None]).astype(jnp.int32), axis=1),
                              N_EXPERTS - 1)
    return src_tok.reshape(n_tiles, 1, tm), dest.reshape(n_tiles, 1, tm), tile_expert, n_tiles


def _experts_kernel(te_ref, src_ref, srcn_ref, dst_ref, h2_hbm, wg_ref, wu_ref, wd_ref, y_hbm,
                    gbuf, obuf, wg_bf, wu_bf, wd_bf, gsem, ssem):
    i = pl.program_id(0)
    n = pl.num_programs(0)
    tm = gbuf.shape[1]
    slot = i % 2

    def start_gather(idx_ref, s):
        def body(k, carry):
            pltpu.make_async_copy(h2_hbm.at[pl.ds(idx_ref[0, 0, k], 1), :], gbuf.at[s, pl.ds(k, 1), :],
                                  gsem.at[s]).start()
            return carry
        lax.fori_loop(0, tm, body, 0, unroll=8)

    def wait_gather(s):
        pltpu.make_async_copy(h2_hbm.at[pl.ds(0, tm), :], gbuf.at[s], gsem.at[s]).wait()

    def wait_scatter(s):
        pltpu.make_async_copy(obuf.at[s], y_hbm.at[pl.ds(0, tm), :], ssem.at[s]).wait()

    @pl.when(i == 0)
    def _():
        start_gather(src_ref, 0)

    @pl.when(i + 1 < n)
    def _():
        start_gather(srcn_ref, 1 - slot)

    changed = (i == 0) | (te_ref[i] != te_ref[jnp.maximum(i - 1, 0)])

    @pl.when(changed)
    def _():
        wg_bf[...] = wg_ref[0].astype(BF16)
        wu_bf[...] = wu_ref[0].astype(BF16)
        wd_bf[...] = wd_ref[0].astype(BF16)

    wait_gather(slot)
    hb = gbuf[slot].astype(BF16)
    gate = jnp.dot(hb, wg_bf[...], preferred_element_type=F32)
    up = jnp.dot(hb, wu_bf[...], preferred_element_type=F32)
    act = (gate * _sigmoid(gate) * up).astype(BF16)
    y = jnp.dot(act, wd_bf[...], preferred_element_type=F32)

    @pl.when(i >= 2)
    def _():
        wait_scatter(slot)

    obuf[slot] = y

    def sbody(k, carry):
        pltpu.make_async_copy(obuf.at[slot, pl.ds(k, 1), :], y_hbm.at[pl.ds(dst_ref[0, 0, k], 1), :],
                              ssem.at[slot]).start()
        return carry
    lax.fori_loop(0, tm, sbody, 0, unroll=8)

    @pl.when(i == n - 1)
    def _():
        wait_scatter(slot)

        @pl.when(n >= 2)
        def _():
            wait_scatter(1 - slot)


def _experts(h2, src_tok, dest, tile_expert, n_tiles, w_gate, w_up, w_down):
    tm = EXPERT_TILE
    t = h2.shape[0]

    def idx_spec(shift):
        return pl.BlockSpec((1, 1, tm), lambda i, te: (jnp.minimum(i + shift, n_tiles - 1), 0, 0),
                            memory_space=pltpu.SMEM)

    def w_spec(shape):
        return pl.BlockSpec((1,) + shape, lambda i, te: (te[i], 0, 0))

    grid_spec = pltpu.PrefetchScalarGridSpec(
        num_scalar_prefetch=1,
        grid=(n_tiles,),
        in_specs=[
            idx_spec(0), idx_spec(1), idx_spec(0),
            pl.BlockSpec(memory_space=pl.ANY),
            w_spec((D_MODEL, D_EXPERT)), w_spec((D_MODEL, D_EXPERT)), w_spec((D_EXPERT, D_MODEL)),
        ],
        out_specs=pl.BlockSpec(memory_space=pl.ANY),
        scratch_shapes=[
            pltpu.VMEM((2, tm, D_MODEL), F32),
            pltpu.VMEM((2, tm, D_MODEL), F32),
            pltpu.VMEM((D_MODEL, D_EXPERT), BF16),
            pltpu.VMEM((D_MODEL, D_EXPERT), BF16),
            pltpu.VMEM((D_EXPERT, D_MODEL), BF16),
            pltpu.SemaphoreType.DMA((2,)),
            pltpu.SemaphoreType.DMA((2,)),
        ],
    )
    return pl.pallas_call(
        _experts_kernel,
        grid_spec=grid_spec,
        out_shape=jax.ShapeDtypeStruct((2 * t + 2 * tm, D_MODEL), F32),
        compiler_params=_cparams("arbitrary"),
        name="experts",
    )(tile_expert, src_tok, src_tok, dest, h2, w_gate, w_up, w_down)


def _final_kernel(x1_ref, ya_ref, yb_ref, rw_ref, nw_ref, o_ref):
    rw = rw_ref[...]
    x = x1_ref[...] + rw[:, 0:1] * ya_ref[...] + rw[:, 1:2] * yb_ref[...]
    o_ref[...] = x * lax.rsqrt(jnp.mean(x * x, axis=-1, keepdims=True) + EPS) * nw_ref[...]


def _final(x1, y, rw, norm_w, row0, n_rows):
    t = x1.shape[0]
    tm = _largest_divisor(np.gcd(np.gcd(row0, n_rows), t), 512)
    b0 = row0 // tm
    bt = t // tm

    return pl.pallas_call(
        _final_kernel,
        grid=(n_rows // tm,),
        in_specs=[
            pl.BlockSpec((tm, D_MODEL), lambda i: (b0 + i, 0)),
            pl.BlockSpec((tm, D_MODEL), lambda i: (b0 + i, 0)),
            pl.BlockSpec((tm, D_MODEL), lambda i: (bt + b0 + i, 0)),
            pl.BlockSpec((tm, LANE), lambda i: (b0 + i, 0)),
            pl.BlockSpec((1, D_MODEL), lambda i: (0, 0)),
        ],
        out_specs=pl.BlockSpec((tm, D_MODEL), lambda i: (i, 0)),
        out_shape=jax.ShapeDtypeStruct((n_rows, D_MODEL), F32),
        compiler_params=_cparams("parallel"),
        name="final",
    )(x1, y, y, rw, norm_w)


def kernel(x_prompt, x_sample, norm1_w, w_in, attn_sink, hg_lb, hg_norm_w, w_out, norm2_w, w_router_g,
           b_router_g, w_router_e, b_router_e, w_gate, w_up, w_down, final_norm_w):
    bp, seq, d = x_prompt.shape
    bs, seq_s, _ = x_sample.shape
    assert d == D_MODEL and seq == seq_s and seq % max(ATTN_BLOCK, HG_CHUNK) == 0
    assert w_in.shape[0] == 1 and w_in.shape[2] == N_SLABS * LANE
    n_seq = bp + bs
    tp, ts = bp * seq, bs * seq
    xp = x_prompt.reshape(tp, d)
    xs = x_sample.reshape(ts, d)

    proj = _inproj(xp, xs, norm1_w, w_in[0].astype(BF16))
    attn = _attention(proj, attn_sink, n_seq, seq)
    hg = _hgrn(proj, hg_lb, hg_norm_w, n_seq, seq)

    wr = jnp.concatenate([w_router_g[0], w_router_e[0].reshape(d, N_EXPERTS)], axis=1)
    wr = jnp.pad(wr, ((0, 0), (0, LANE - wr.shape[1])))
    wr_hi = wr.astype(BF16)
    wr_lo = (wr - wr_hi.astype(F32)).astype(BF16)
    br = jnp.concatenate([b_router_g[0], b_router_e[0].reshape(N_EXPERTS)])
    br = jnp.pad(br, (0, LANE - br.shape[0])).reshape(1, LANE)
    x1, h2, ri, rw, cnt = _outproj(xp, xs, attn, hg, w_out[0].astype(BF16), norm2_w, wr_hi, wr_lo, br)

    counts = cnt[0, N_GROUPS:N_GROUPS + N_EXPERTS]
    src_tok, dest, tile_expert, n_tiles = _plan_expert_tiles(ri, counts, EXPERT_TILE)
    y = _experts(h2, src_tok, dest, tile_expert, n_tiles, w_gate[0], w_up[0], w_down[0])

    fnw = final_norm_w.reshape(1, d)
    y_prompt = _final(x1, y, rw, fnw, 0, tp).reshape(bp, seq, d)
    y_sample = _final(x1, y, rw, fnw, tp, ts).reshape(bs, seq, d)
    return (y_prompt, y_sample)
```

```python
import functools

import numpy as np
import jax
import jax.numpy as jnp
from jax import lax
from jax.experimental import pallas as pl
from jax.experimental.pallas import tpu as pltpu

F32 = jnp.float32
BF16 = jnp.bfloat16

LANE = 128
D_MODEL = 2048
ATTN_HEADS = 8
ATTN_KV_HEADS = 2
ATTN_GROUP = ATTN_HEADS // ATTN_KV_HEADS
HEAD_DIM = LANE
WINDOW = 128
ATTN_BLOCK = 128
HG_HEADS = 8
HG_CHUNK = 128
HG_VPU_MIN_HALF = 4
HG_HEADS_PER_STEP = 4
N_GROUPS = 4
EXPERTS_PER_GROUP = 8
N_EXPERTS = N_GROUPS * EXPERTS_PER_GROUP
D_EXPERT = 512
EPS = 1e-6

SLAB_QA, SLAB_KA, SLAB_VA, SLAB_QH, SLAB_FF, SLAB_FB, SLAB_IH, SLAB_GH = 0, 8, 10, 12, 20, 28, 36, 44
N_SLABS = 52

EXPERT_TILE = 256
ROUTE_TILE = 256
ROUTE_SUB = 128
VMEM_LIMIT = 56 * 1024 * 1024


def _largest_divisor(n, cap):
    n = int(n)
    d = min(n, cap)
    while n % d:
        d -= 1
    return d


def _cparams(*sem):
    return pltpu.CompilerParams(dimension_semantics=sem, vmem_limit_bytes=VMEM_LIMIT)


def _inproj_kernel(nbp, xp_ref, xs_ref, nw_ref, w_ref, o_ref, h_ref):
    i = pl.program_id(0)
    j = pl.program_id(1)

    def norm_into_scratch(x_ref):
        x = x_ref[...]
        ms = jnp.mean(x * x, axis=-1, keepdims=True)
        h_ref[...] = (x * lax.rsqrt(ms + EPS) * nw_ref[...]).astype(BF16)

    @pl.when((j == 0) & (i < nbp))
    def _():
        norm_into_scratch(xp_ref)

    @pl.when((j == 0) & (i >= nbp))
    def _():
        norm_into_scratch(xs_ref)

    acc = jnp.dot(h_ref[...], w_ref[...], preferred_element_type=F32)
    for c in range(o_ref.shape[0]):
        o_ref[c] = acc[:, c * LANE:(c + 1) * LANE].astype(BF16)


def _inproj(xp, xs, norm_w, w_bf16):
    tp, ts = xp.shape[0], xs.shape[0]
    tm = _largest_divisor(np.gcd(tp, ts), 1024)
    tn = 512
    nbp, nbs = tp // tm, ts // tm
    n_out = w_bf16.shape[1]
    return pl.pallas_call(
        functools.partial(_inproj_kernel, nbp),
        grid=(nbp + nbs, n_out // tn),
        in_specs=[
            pl.BlockSpec((tm, D_MODEL), lambda i, j: (jnp.minimum(i, nbp - 1), 0)),
            pl.BlockSpec((tm, D_MODEL), lambda i, j: (jnp.maximum(i - nbp, 0), 0)),
            pl.BlockSpec((1, D_MODEL), lambda i, j: (0, 0)),
            pl.BlockSpec((D_MODEL, tn), lambda i, j: (0, j)),
        ],
        out_specs=pl.BlockSpec((tn // LANE, tm, LANE), lambda i, j: (j, i, 0)),
        out_shape=jax.ShapeDtypeStruct((n_out // LANE, tp + ts, LANE), BF16),
        scratch_shapes=[pltpu.VMEM((tm, D_MODEL), BF16)],
        compiler_params=_cparams("parallel", "arbitrary"),
        name="inproj",
    )(xp, xs, norm_w, w_bf16)


LOG2E = 1.4426950408889634


def _attn_bias_table():
    blk = ATTN_BLOCK
    qi = np.arange(blk)[:, None]
    kc = np.arange(3 * blk)[None, :]
    dist = np.abs(kc - blk - qi)
    slopes = 2.0 ** (-8.0 * np.arange(1, ATTN_HEADS + 1) / ATTN_HEADS)
    table = np.empty((4, ATTN_HEADS, blk, 3 * blk), np.float32)
    for first in (0, 1):
        for last in (0, 1):
            ok = (dist <= WINDOW) & ((kc >= blk) | (first == 0)) & ((kc < 2 * blk) | (last == 0))
            for h in range(ATTN_HEADS):
                table[2 * first + last, h] = np.where(ok, -slopes[h] * LOG2E * dist, -np.inf)
    return table


def _attn_kernel(q_ref, kp_ref, kc_ref, kn_ref, vp_ref, vc_ref, vn_ref, sink_ref, tab_ref, o_ref):
    n = pl.program_id(1)
    nb = pl.num_programs(1)
    edge = 2 * (n == 0).astype(jnp.int32) + (n == nb - 1).astype(jnp.int32)
    scale2 = HEAD_DIM ** -0.5 * LOG2E
    for g in range(ATTN_KV_HEADS):
        heads = range(g * ATTN_GROUP, (g + 1) * ATTN_GROUP)
        kcat = jnp.concatenate([kp_ref[g], kc_ref[g], kn_ref[g]], axis=0)
        vcat = jnp.concatenate([vp_ref[g], vc_ref[g], vn_ref[g]], axis=0)
        logits = [lax.dot_general(q_ref[h], kcat, (((1,), (1,)), ((), ())), preferred_element_type=F32)
                  * scale2 + tab_ref[edge, h] for h in heads]
        probs = []
        for h, s in zip(heads, logits):
            sink2 = sink_ref[0, h] * LOG2E
            m = jnp.maximum(jnp.max(s, axis=-1, keepdims=True), sink2)
            p = jnp.exp2(s - m)
            denom = jnp.sum(p, axis=-1, keepdims=True) + jnp.exp2(sink2 - m)
            probs.append((p.astype(BF16), 1.0 / denom))
        for h, (p, inv) in zip(heads, probs):
            o = jnp.dot(p, vcat, preferred_element_type=F32) * inv
            o_ref[:, h * HEAD_DIM:(h + 1) * HEAD_DIM] = o.astype(BF16)


def _attention(proj, sink, n_seq, seq):
    nb = seq // ATTN_BLOCK
    t = n_seq * seq
    table = jnp.asarray(_attn_bias_table())

    def kv_spec(slab_pair, shift):
        def imap(b, n):
            return (slab_pair, b * nb + jnp.clip(n + shift, 0, nb - 1), 0)
        return pl.BlockSpec((ATTN_KV_HEADS, ATTN_BLOCK, LANE), imap)

    return pl.pallas_call(
        _attn_kernel,
        grid=(n_seq, nb),
        in_specs=[
            pl.BlockSpec((ATTN_HEADS, ATTN_BLOCK, LANE), lambda b, n: (SLAB_QA // ATTN_HEADS, b * nb + n, 0)),
            kv_spec(SLAB_KA // ATTN_KV_HEADS, -1), kv_spec(SLAB_KA // ATTN_KV_HEADS, 0),
            kv_spec(SLAB_KA // ATTN_KV_HEADS, 1),
            kv_spec(SLAB_VA // ATTN_KV_HEADS, -1), kv_spec(SLAB_VA // ATTN_KV_HEADS, 0),
            kv_spec(SLAB_VA // ATTN_KV_HEADS, 1),
            pl.BlockSpec(memory_space=pltpu.SMEM),
            pl.BlockSpec(table.shape, lambda b, n: (0, 0, 0, 0)),
        ],
        out_specs=pl.BlockSpec((ATTN_BLOCK, ATTN_HEADS * HEAD_DIM), lambda b, n: (b * nb + n, 0)),
        out_shape=jax.ShapeDtypeStruct((t, ATTN_HEADS * HEAD_DIM), BF16),
        compiler_params=_cparams("parallel", "arbitrary"),
        name="attn",
    )(proj, proj, proj, proj, proj, proj, proj, sink, table)


def _hgrn_halves(c):
    halves = []
    h = c // 2
    while h >= 1:
        halves.append(h)
        h //= 2
    return halves


def _hgrn_constants(c, reverse):
    halves = _hgrn_halves(c)
    m = np.zeros((2, c, c), np.float32)
    level = -np.ones((c, c), np.int32)
    sign = np.zeros((sum(h >= HG_VPU_MIN_HALF for h in halves), c, LANE), np.float32)
    for t in range(c):
        m[0, t, :t + 1] = 1
        level[t, t] = 0
    for li, h in enumerate(halves):
        for t in range(c):
            pos = t % (2 * h)
            mid = t - pos + h
            if pos >= h:
                level[t, t - pos:mid] = li + 1
            if h == 2:
                if pos >= h:
                    m[1, t, mid:t + 1] = 1
                else:
                    m[1, t, t + 1:mid] = 1
            if h >= HG_VPU_MIN_HALF:
                sign[li, t, :] = 1.0 if pos >= h else -1.0
    if reverse:
        m = m[:, ::-1, ::-1]
        level = level[::-1, ::-1]
        sign = sign[:, ::-1, :]
    m = m.reshape(-1, c)
    return (np.ascontiguousarray(np.concatenate([m, m], axis=1)), np.ascontiguousarray(level),
            np.ascontiguousarray(sign))


def _sigmoid(x):
    return 0.5 + 0.5 * jnp.tanh(0.5 * x)


def _hgrn_kernel(c, q_ref, ff_ref, fb_ref, i_ref, g_ref, lb_ref, nw_ref,
                 mf_ref, mb_ref, lvf_ref, lvb_ref, sgf_ref, sgb_ref, o_ref, of_ref, ob_ref, qs_ref, st_ref):
    n_heads, seq = q_ref.shape[0], q_ref.shape[1]
    nc = seq // c
    halves = _hgrn_halves(c)
    row = lax.broadcasted_iota(jnp.int32, (c, LANE), 0)
    sub = 8

    lbr = lb_ref[...]
    lbe = jnp.exp(lbr - jnp.max(lbr, axis=0, keepdims=True))
    lb_all = lbe[0] / jnp.sum(lbe, axis=0)

    st_ref[...] = jnp.zeros_like(st_ref)

    def qs_body(it, carry):
        r0 = pl.multiple_of(it * c, c)
        for hd in range(n_heads):
            qp = q_ref[hd, pl.ds(r0, c), :].astype(F32)
            qs_ref[hd, pl.ds(r0, c), :] = qp * _sigmoid(qp)
        return carry

    lax.fori_loop(0, nc, qs_body, 0)

    def gates(ch):
        hd, r0, z_ref, m_ref, _, _, reverse = ch
        d = 1 if reverse else 0
        lb = lb_all[d:d + 1, hd * LANE:(hd + 1) * LANE]
        z = z_ref[hd, pl.ds(r0, c), :].astype(F32)
        sig = _sigmoid(z)
        f = lb + (1.0 - lb) * sig
        k = (1.0 - lb) * (1.0 - sig)
        g = jnp.log2(f)
        g_hi = g.astype(BF16)
        g_lo = (g - g_hi.astype(F32)).astype(BF16)
        e = jnp.dot(m_ref[...], jnp.concatenate([g_hi, g_lo], axis=0), preferred_element_type=F32)
        return dict(f=f, k=k, e=e, b=e[0:c, :], qs=qs_ref[hd, pl.ds(r0, c), :])

    def level(ch, s, li):
        _, _, _, _, lv_ref, sg_ref, reverse = ch
        qs, k, b = s["qs"], s["k"], s["b"]
        if li < 0:
            x_q, x_k = qs.astype(BF16), k.astype(BF16)
            h = 1
        else:
            h = halves[li]
            if h == 1:
                w = jnp.where((row & 1) == (0 if reverse else 1), s["f"], 1.0)
            elif h == 2:
                w = jnp.exp2(s["e"][c:2 * c, :])
            elif h < sub:
                b3 = b.reshape(c // (2 * h), 2 * h, LANE)
                mid = h if reverse else h - 1
                w = jnp.exp2((b3 - b3[:, mid:mid + 1, :]).reshape(c, LANE) * sg_ref[li])
            else:
                parts = []
                for j in range(c // h):
                    mid = (j // 2) * 2 * h + (h if reverse else h - 1)
                    blk, ref_row = b[j * h:(j + 1) * h], b[mid:mid + 1]
                    parts.append(blk - ref_row if (j % 2 == 1) != reverse else ref_row - blk)
                w = jnp.exp2(jnp.concatenate(parts, axis=0))
            if h >= sub:
                base = jnp.concatenate(
                    [(qs if (j % 2 == 1) != reverse else k)[j * h:(j + 1) * h] for j in range(c // h)], axis=0)
            else:
                base = jnp.where((row & h) == (0 if reverse else h), qs, k)
            x_q = x_k = (base * w).astype(BF16)
        p = lax.dot_general(x_q, x_k, (((1,), (1,)), ((), ())), preferred_element_type=F32)
        a = s.get("a")
        new_a = []
        for j in range(c // sub):
            rows = slice(j * sub, (j + 1) * sub)
            if a is None:
                new_a.append(jnp.where(lv_ref[rows, :] == 0, p[rows], 0.0))
            elif h < sub or ((j * sub // h) % 2 == 1) != reverse:
                new_a.append(jnp.where(lv_ref[rows, :] == li + 1, p[rows], a[j]))
            else:
                new_a.append(a[j])
        s["a"] = new_a

    def apply(ch, s):
        hd, r0, _, _, _, _, reverse = ch
        d = 1 if reverse else 0
        qs, k, b = s["qs"], s["k"], s["b"]
        v = i_ref[hd, pl.ds(r0, c), :]
        last = 0 if reverse else c - 1
        b_last = b[last:last + 1, :]
        st = st_ref[hd, d]
        o = jnp.dot(jnp.concatenate(s["a"], axis=0).astype(BF16), v, preferred_element_type=F32)
        o = o + lax.dot_general((qs * jnp.exp2(b)).astype(BF16), st.astype(BF16), (((1,), (1,)), ((), ())),
                                preferred_element_type=F32)
        k_out = (k * jnp.exp2(b_last - b)).astype(BF16)
        st_ref[hd, d] = st * jnp.exp2(b_last) + lax.dot_general(
            v, k_out, (((0,), (0,)), ((), ())), preferred_element_type=F32)
        return o

    def scan_body(it, carry):
        rf = pl.multiple_of(it * c, c)
        rb = pl.multiple_of((nc - 1 - it) * c, c)
        chains = []
        for hd in range(n_heads):
            chains.append((hd, rf, ff_ref, mf_ref, lvf_ref, sgf_ref, False))
            chains.append((hd, rb, fb_ref, mb_ref, lvb_ref, sgb_ref, True))
        states = [gates(ch) for ch in chains]
        for li in range(-1, len(halves)):
            for ch, s in zip(chains, states):
                level(ch, s, li)
        for ch, s in zip(chains, states):
            out_ref = ob_ref if ch[6] else of_ref
            out_ref[ch[0], pl.ds(ch[1], c), :] = apply(ch, s)
        return carry

    lax.fori_loop(0, nc, scan_body, 0)

    def out_body(it, carry):
        r0 = pl.multiple_of(it * c, c)
        for hd in range(n_heads):
            tot = of_ref[hd, pl.ds(r0, c), :] + ob_ref[hd, pl.ds(r0, c), :]
            y = tot * lax.rsqrt(jnp.mean(tot * tot, axis=-1, keepdims=True) + EPS) * nw_ref[...]
            gp = g_ref[hd, pl.ds(r0, c), :].astype(F32)
            o_ref[pl.ds(r0, c), hd * LANE:(hd + 1) * LANE] = (y * (gp * _sigmoid(gp))).astype(BF16)
        return carry

    lax.fori_loop(0, nc, out_body, 0)


def _hgrn(proj, hg_lb, norm_w, n_seq, seq):
    c = HG_CHUNK
    hh = HG_HEADS_PER_STEP
    mf, lvf, sgf = _hgrn_constants(c, False)
    mb, lvb, sgb = _hgrn_constants(c, True)
    t = n_seq * seq

    def slab(first):
        return pl.BlockSpec((hh, seq, LANE), lambda b, h: (first // hh + h, b, 0))

    def const(arr):
        return pl.BlockSpec(arr.shape, lambda b, h: (0,) * arr.ndim)

    return pl.pallas_call(
        functools.partial(_hgrn_kernel, c),
        grid=(n_seq, HG_HEADS // hh),
        in_specs=[
            slab(SLAB_QH), slab(SLAB_FF), slab(SLAB_FB), slab(SLAB_IH), slab(SLAB_GH),
            pl.BlockSpec((hg_lb.shape[0], 2, hh * LANE), lambda b, h: (0, 0, h)),
            pl.BlockSpec((1, LANE), lambda b, h: (0, 0)),
            const(mf), const(mb), const(lvf), const(lvb), const(sgf), const(sgb),
        ],
        out_specs=pl.BlockSpec((seq, hh * LANE), lambda b, h: (b, h)),
        out_shape=jax.ShapeDtypeStruct((t, HG_HEADS * LANE), BF16),
        scratch_shapes=[pltpu.VMEM((hh, seq, LANE), F32)] * 3 + [pltpu.VMEM((hh, 2, LANE, LANE), F32)],
        compiler_params=_cparams("parallel", "arbitrary"),
        name="hgrn",
    )(proj, proj, proj, proj, proj, hg_lb, norm_w,
      jnp.asarray(mf, BF16), jnp.asarray(mb, BF16), jnp.asarray(lvf), jnp.asarray(lvb),
      jnp.asarray(sgf), jnp.asarray(sgb))


def _outproj_kernel(nbp, xp_ref, xs_ref, a_ref, hg_ref, wo_ref, n2_ref, wrh_ref, wrl_ref, br_ref, tri_ref,
                    x1_ref, h2_ref, ri_ref, rw_ref, cnt_ref, run_ref):
    i = pl.program_id(0)
    half = a_ref.shape[1]
    sub_rows = tri_ref.shape[0]
    n_sub = a_ref.shape[0] // sub_rows

    @pl.when(i == 0)
    def _():
        run_ref[...] = jnp.zeros_like(run_ref)

    groups = [slice(j * sub_rows, (j + 1) * sub_rows) for j in range(n_sub)]
    from_prompt = i < nbp

    x1s = []
    for rows in groups:
        acc = jnp.dot(a_ref[rows, :], wo_ref[0:half, :], preferred_element_type=F32)
        acc = acc + jnp.dot(hg_ref[rows, :], wo_ref[half:2 * half, :], preferred_element_type=F32)
        x1 = jnp.where(from_prompt, xp_ref[rows, :], xs_ref[rows, :]) + acc
        x1_ref[rows, :] = x1
        x1s.append(x1)

    logits_all = []
    for rows, x1 in zip(groups, x1s):
        h2 = x1 * lax.rsqrt(jnp.mean(x1 * x1, axis=-1, keepdims=True) + EPS) * n2_ref[...]
        h2_ref[rows, :] = h2
        h_hi = h2.astype(BF16)
        h_lo = (h2 - h_hi.astype(F32)).astype(BF16)
        logits_all.append((jnp.dot(h_hi, wrh_ref[...], preferred_element_type=F32)
                           + jnp.dot(h_lo, wrh_ref[...], preferred_element_type=F32)
                           + jnp.dot(h_hi, wrl_ref[...], preferred_element_type=F32)) + br_ref[...])

    lane = lax.broadcasted_iota(jnp.int32, logits_all[0].shape, 1).astype(F32)
    far = float(LANE)
    is_g = lane < N_GROUPS
    picks = []
    for logits in logits_all:
        gl = jnp.where(is_g, logits, -jnp.inf)
        gmax = jnp.max(gl, axis=-1, keepdims=True)
        gidx = jnp.min(jnp.where(gl == gmax, lane, far), axis=-1, keepdims=True)
        g_w = 1.0 / jnp.sum(jnp.where(is_g, jnp.exp(logits - gmax), 0.0), axis=-1, keepdims=True)
        lo = N_GROUPS + gidx * EXPERTS_PER_GROUP
        in_grp = (lane >= lo) & (lane < lo + EXPERTS_PER_GROUP)
        el = jnp.where(in_grp, logits, -jnp.inf)
        m1 = jnp.max(el, axis=-1, keepdims=True)
        i1 = jnp.min(jnp.where(el == m1, lane, far), axis=-1, keepdims=True)
        el2 = jnp.where(lane == i1, -jnp.inf, el)
        m2 = jnp.max(el2, axis=-1, keepdims=True)
        i2 = jnp.min(jnp.where(el2 == m2, lane, far), axis=-1, keepdims=True)
        p2 = jnp.exp(m2 - m1)
        picks.append((i1, i2, g_w * (1.0 / (1.0 + p2)), g_w * (p2 / (1.0 + p2))))

    run = run_ref[...]
    for rows, (i1, i2, c1, c2) in zip(groups, picks):
        oh1 = lane == i1
        oh2 = lane == i2
        one1 = jnp.where(oh1, 1.0, 0.0)
        one2 = jnp.where(oh2, 1.0, 0.0)
        before1 = jnp.dot(tri_ref[...], one1.astype(BF16), preferred_element_type=F32)
        before2 = jnp.dot(tri_ref[...], one2.astype(BF16), preferred_element_type=F32)
        tot1 = jnp.sum(one1, axis=0, keepdims=True)
        r1 = jnp.sum(jnp.where(oh1, run + before1, 0.0), axis=-1, keepdims=True)
        r2 = jnp.sum(jnp.where(oh2, run + tot1 + before2, 0.0), axis=-1, keepdims=True)
        run = run + tot1 + jnp.sum(one2, axis=0, keepdims=True)
        ri = jnp.where(lane == 0.0, i1 - N_GROUPS, jnp.where(lane == 1.0, i2 - N_GROUPS,
                       jnp.where(lane == 2.0, r1, jnp.where(lane == 3.0, r2, 0.0))))
        ri_ref[rows, :] = ri.astype(jnp.int32)
        rw_ref[rows, :] = jnp.where(lane == 0.0, c1, jnp.where(lane == 1.0, c2, 0.0))
    run_ref[...] = run
    cnt_ref[...] = run.astype(jnp.int32)


def _outproj(xp, xs, attn, hg, wo_bf16, norm2_w, wr_hi, wr_lo, br):
    tp, ts = xp.shape[0], xs.shape[0]
    t = tp + ts
    tm = _largest_divisor(np.gcd(tp, ts), ROUTE_TILE)
    nbp, nbs = tp // tm, ts // tm
    half = attn.shape[1]
    sub_rows = _largest_divisor(tm, ROUTE_SUB)
    strict_lower = jnp.asarray(np.tril(np.ones((sub_rows, sub_rows), np.float32), -1), BF16)

    def const(shape):
        return pl.BlockSpec(shape, lambda i: (0, 0))

    def rows(width):
        return pl.BlockSpec((tm, width), lambda i: (i, 0))

    return pl.pallas_call(
        functools.partial(_outproj_kernel, nbp),
        grid=(nbp + nbs,),
        in_specs=[
            pl.BlockSpec((tm, D_MODEL), lambda i: (jnp.minimum(i, nbp - 1), 0)),
            pl.BlockSpec((tm, D_MODEL), lambda i: (jnp.maximum(i - nbp, 0), 0)),
            rows(half), rows(half),
            const((2 * half, D_MODEL)), const((1, D_MODEL)),
            const((D_MODEL, LANE)), const((D_MODEL, LANE)), const((1, LANE)), const((sub_rows, sub_rows)),
        ],
        out_specs=[rows(D_MODEL), rows(D_MODEL), rows(LANE), rows(LANE), const((1, LANE))],
        out_shape=[
            jax.ShapeDtypeStruct((t, D_MODEL), F32),
            jax.ShapeDtypeStruct((t, D_MODEL), F32),
            jax.ShapeDtypeStruct((t, LANE), jnp.int32),
            jax.ShapeDtypeStruct((t, LANE), F32),
            jax.ShapeDtypeStruct((1, LANE), jnp.int32),
        ],
        scratch_shapes=[pltpu.VMEM((1, LANE), F32)],
        compiler_params=_cparams("arbitrary"),
        name="outproj",
    )(xp, xs, attn, hg, wo_bf16, norm2_w, wr_hi, wr_lo, br, strict_lower)


def _plan_expert_tiles(ri, counts, tm):
    t = ri.shape[0]
    n_slots = 2 * t
    n_tiles = n_slots // tm + N_EXPERTS
    p = n_tiles * tm
    ef = jnp.concatenate([ri[:, 0], ri[:, 1]])
    rank = jnp.concatenate([ri[:, 2], ri[:, 3]])
    tiles_per = (counts + tm - 1) // tm
    tile_end = jnp.cumsum(tiles_per)
    seg_start = (tile_end - tiles_per) * tm
    experts = jnp.arange(N_EXPERTS, dtype=jnp.int32)
    ppos = jnp.sum(jnp.where(ef[:, None] == experts[None, :], seg_start[None, :], 0), axis=1) + rank
    slot = jnp.arange(n_slots, dtype=jnp.int32)
    real_dest = jnp.full((p,), -1, jnp.int32).at[ppos].set(slot, unique_indices=True)
    is_pad = real_dest < 0
    pos = jnp.arange(p, dtype=jnp.int32)
    dest = jnp.where(is_pad, n_slots + pos % (2 * tm), real_dest)
    src_tok = jnp.where(is_pad, 0, real_dest % t)
    tiles = jnp.arange(n_tiles, dtype=jnp.int32)
    tile_expert = jnp.minimum(jnp.sum((tile_end[None, :] <= tiles[:, None]).astype(jnp.int32), axis=1),
                              N_EXPERTS - 1)
    return src_tok.reshape(n_tiles, 1, tm), dest.reshape(n_tiles, 1, tm), tile_expert, n_tiles


def _experts_kernel(te_ref, src_ref, srcn_ref, dstp_ref, dst_ref, h2_hbm, wg_ref, wu_ref, wd_ref, y_hbm,
                    gbuf, obuf, wg_bf, wu_bf, wd_bf, gsem, ssem):
    i = pl.program_id(0)
    n = pl.num_programs(0)
    tm = gbuf.shape[1]
    slot = i % 2
    other = 1 - slot

    def gather_row(idx_ref, k, s):
        return pltpu.make_async_copy(h2_hbm.at[pl.ds(idx_ref[0, 0, k], 1), :], gbuf.at[s, pl.ds(k, 1), :],
                                     gsem.at[s])

    def scatter_row(idx_ref, k, s):
        return pltpu.make_async_copy(obuf.at[s, pl.ds(k, 1), :], y_hbm.at[pl.ds(idx_ref[0, 0, k], 1), :],
                                     ssem.at[s])

    def wait_gather(s):
        pltpu.make_async_copy(h2_hbm.at[pl.ds(0, tm), :], gbuf.at[s], gsem.at[s]).wait()

    def wait_scatter(s):
        pltpu.make_async_copy(obuf.at[s], y_hbm.at[pl.ds(0, tm), :], ssem.at[s]).wait()

    @pl.when(i == 0)
    def _():
        def body(k, carry):
            gather_row(src_ref, k, 0).start()
            return carry
        lax.fori_loop(0, tm, body, 0, unroll=8)
        obuf[1] = jnp.zeros(obuf.shape[1:], obuf.dtype)

    @pl.when(i >= 1)
    def _():
        wait_scatter(slot)

    changed = (i == 0) | (te_ref[i] != te_ref[jnp.maximum(i - 1, 0)])

    @pl.when(changed)
    def _():
        wg_bf[...] = wg_ref[0].astype(BF16)
        wu_bf[...] = wu_ref[0].astype(BF16)
        wd_bf[...] = wd_ref[0].astype(BF16)

    def tile_body(s):
        wait_gather(s)
        hb = gbuf[s].astype(BF16)
        for k in range(tm):
            gather_row(srcn_ref, k, 1 - s).start()
        for k in range(tm):
            scatter_row(dstp_ref, k, 1 - s).start()
        gate = jnp.dot(hb, wg_bf[...], preferred_element_type=F32)
        up = jnp.dot(hb, wu_bf[...], preferred_element_type=F32)
        act = (gate * _sigmoid(gate) * up).astype(BF16)
        obuf[s] = jnp.dot(act, wd_bf[...], preferred_element_type=F32)

    @pl.when(slot == 0)
    def _():
        tile_body(0)

    @pl.when(slot == 1)
    def _():
        tile_body(1)

    @pl.when(i == n - 1)
    def _():
        wait_gather(other)
        wait_scatter(other)

        def body(k, carry):
            scatter_row(dst_ref, k, slot).start()
            return carry
        lax.fori_loop(0, tm, body, 0, unroll=8)
        wait_scatter(slot)


def _experts(h2, src_tok, dest, tile_expert, n_tiles, w_gate, w_up, w_down):
    tm = EXPERT_TILE
    t = h2.shape[0]

    def idx_spec(shift):
        return pl.BlockSpec((1, 1, tm), lambda i, te: (jnp.minimum(i + shift, n_tiles - 1), 0, 0),
                            memory_space=pltpu.SMEM)

    def w_spec(shape):
        return pl.BlockSpec((1,) + shape, lambda i, te: (te[i], 0, 0))

    spare = (2 * t + jnp.arange(tm, dtype=jnp.int32)).reshape(1, 1, tm)
    dest_prev = jnp.concatenate([spare, dest[:-1]], axis=0)

    grid_spec = pltpu.PrefetchScalarGridSpec(
        num_scalar_prefetch=1,
        grid=(n_tiles,),
        in_specs=[
            idx_spec(0), idx_spec(1), idx_spec(0), idx_spec(0),
            pl.BlockSpec(memory_space=pl.ANY),
            w_spec((D_MODEL, D_EXPERT)), w_spec((D_MODEL, D_EXPERT)), w_spec((D_EXPERT, D_MODEL)),
        ],
        out_specs=pl.BlockSpec(memory_space=pl.ANY),
        scratch_shapes=[
            pltpu.VMEM((2, tm, D_MODEL), F32),
            pltpu.VMEM((2, tm, D_MODEL), F32),
            pltpu.VMEM((D_MODEL, D_EXPERT), BF16),
            pltpu.VMEM((D_MODEL, D_EXPERT), BF16),
            pltpu.VMEM((D_EXPERT, D_MODEL), BF16),
            pltpu.SemaphoreType.DMA((2,)),
            pltpu.SemaphoreType.DMA((2,)),
        ],
    )
    return pl.pallas_call(
        _experts_kernel,
        grid_spec=grid_spec,
        out_shape=jax.ShapeDtypeStruct((2 * t + 2 * tm, D_MODEL), F32),
        compiler_params=_cparams("arbitrary"),
        name="experts",
    )(tile_expert, src_tok, src_tok, dest_prev, dest, h2, w_gate, w_up, w_down)


def _final_kernel(x1_ref, ya_ref, yb_ref, rw_ref, nw_ref, o_ref):
    rw = rw_ref[...]
    x = x1_ref[...] + rw[:, 0:1] * ya_ref[...] + rw[:, 1:2] * yb_ref[...]
    o_ref[...] = x * lax.rsqrt(jnp.mean(x * x, axis=-1, keepdims=True) + EPS) * nw_ref[...]


def _final(x1, y, rw, norm_w, row0, n_rows):
    t = x1.shape[0]
    tm = _largest_divisor(np.gcd(np.gcd(row0, n_rows), t), 512)
    b0 = row0 // tm
    bt = t // tm

    return pl.pallas_call(
        _final_kernel,
        grid=(n_rows // tm,),
        in_specs=[
            pl.BlockSpec((tm, D_MODEL), lambda i: (b0 + i, 0)),
            pl.BlockSpec((tm, D_MODEL), lambda i: (b0 + i, 0)),
            pl.BlockSpec((tm, D_MODEL), lambda i: (bt + b0 + i, 0)),
            pl.BlockSpec((tm, LANE), lambda i: (b0 + i, 0)),
            pl.BlockSpec((1, D_MODEL), lambda i: (0, 0)),
        ],
        out_specs=pl.BlockSpec((tm, D_MODEL), lambda i: (i, 0)),
        out_shape=jax.ShapeDtypeStruct((n_rows, D_MODEL), F32),
        compiler_params=_cparams("parallel"),
        name="final",
    )(x1, y, y, rw, norm_w)


def kernel(x_prompt, x_sample, norm1_w, w_in, attn_sink, hg_lb, hg_norm_w, w_out, norm2_w, w_router_g,
           b_router_g, w_router_e, b_router_e, w_gate, w_up, w_down, final_norm_w):
    bp, seq, d = x_prompt.shape
    bs, seq_s, _ = x_sample.shape
    assert d == D_MODEL and seq == seq_s and seq % max(ATTN_BLOCK, HG_CHUNK) == 0
    assert w_in.shape[0] == 1 and w_in.shape[2] == N_SLABS * LANE
    n_seq = bp + bs
    tp, ts = bp * seq, bs * seq
    xp = x_prompt.reshape(tp, d)
    xs = x_sample.reshape(ts, d)

    proj = _inproj(xp, xs, norm1_w, w_in[0].astype(BF16))
    attn = _attention(proj, attn_sink, n_seq, seq)
    hg = _hgrn(proj, hg_lb, hg_norm_w, n_seq, seq)

    wr = jnp.concatenate([w_router_g[0], w_router_e[0].reshape(d, N_EXPERTS)], axis=1)
    wr = jnp.pad(wr, ((0, 0), (0, LANE - wr.shape[1])))
    wr_hi = wr.astype(BF16)
    wr_lo = (wr - wr_hi.astype(F32)).astype(BF16)
    br = jnp.concatenate([b_router_g[0], b_router_e[0].reshape(N_EXPERTS)])
    br = jnp.pad(br, (0, LANE - br.shape[0])).reshape(1, LANE)
    x1, h2, ri, rw, cnt = _outproj(xp, xs, attn, hg, w_out[0].astype(BF16), norm2_w, wr_hi, wr_lo, br)

    counts = cnt[0, N_GROUPS:N_GROUPS + N_EXPERTS]
    src_tok, dest, tile_expert, n_tiles = _plan_expert_tiles(ri, counts, EXPERT_TILE)
    y = _experts(h2, src_tok, dest, tile_expert, n_tiles, w_gate[0], w_up[0], w_down[0])

    fnw = final_norm_w.reshape(1, d)
    y_prompt = _final(x1, y, rw, fnw, 0, tp).reshape(bp, seq, d)
    y_sample = _final(x1, y, rw, fnw, tp, ts).reshape(bs, seq, d)
    return (y_prompt, y_sample)
```

```python
import functools

import numpy as np
import jax
import jax.numpy as jnp
from jax import lax
from jax.experimental import pallas as pl
from jax.experimental.pallas import tpu as pltpu

F32 = jnp.float32
BF16 = jnp.bfloat16

LANE = 128
D_MODEL = 2048
ROW_PLANES = D_MODEL // LANE
ATTN_HEADS = 8
ATTN_KV_HEADS = 2
ATTN_GROUP = ATTN_HEADS // ATTN_KV_HEADS
HEAD_DIM = LANE
WINDOW = 128
ATTN_BLOCK = 128
HG_HEADS = 8
HG_CHUNK = 128
HG_VPU_MIN_HALF = 4
HG_HEADS_PER_STEP = 4
N_GROUPS = 4
EXPERTS_PER_GROUP = 8
N_EXPERTS = N_GROUPS * EXPERTS_PER_GROUP
D_EXPERT = 512
EPS = 1e-6

SLAB_QA, SLAB_KA, SLAB_VA, SLAB_QH, SLAB_FF, SLAB_FB, SLAB_IH, SLAB_GH = 0, 8, 10, 12, 20, 28, 36, 44
N_SLABS = 52

EXPERT_TILE = 256
EXPERT_ROW_PITCH = 24
ROUTE_TILE = 256
ROUTE_SUB = 128
VMEM_LIMIT = 56 * 1024 * 1024


def _largest_divisor(n, cap):
    n = int(n)
    d = min(n, cap)
    while n % d:
        d -= 1
    return d


def _cparams(*sem):
    return pltpu.CompilerParams(dimension_semantics=sem, vmem_limit_bytes=VMEM_LIMIT)


def _inproj_kernel(nbp, xp_ref, xs_ref, nw_ref, w_ref, o_ref, h_ref):
    i = pl.program_id(0)
    j = pl.program_id(1)

    def norm_into_scratch(x_ref):
        x = x_ref[...]
        ms = jnp.mean(x * x, axis=-1, keepdims=True)
        h_ref[...] = (x * lax.rsqrt(ms + EPS) * nw_ref[...]).astype(BF16)

    @pl.when((j == 0) & (i < nbp))
    def _():
        norm_into_scratch(xp_ref)

    @pl.when((j == 0) & (i >= nbp))
    def _():
        norm_into_scratch(xs_ref)

    acc = jnp.dot(h_ref[...], w_ref[...], preferred_element_type=F32)
    for c in range(o_ref.shape[0]):
        o_ref[c] = acc[:, c * LANE:(c + 1) * LANE].astype(BF16)


def _inproj(xp, xs, norm_w, w_bf16):
    tp, ts = xp.shape[0], xs.shape[0]
    tm = _largest_divisor(np.gcd(tp, ts), 1024)
    tn = 512
    nbp, nbs = tp // tm, ts // tm
    n_out = w_bf16.shape[1]
    return pl.pallas_call(
        functools.partial(_inproj_kernel, nbp),
        grid=(nbp + nbs, n_out // tn),
        in_specs=[
            pl.BlockSpec((tm, D_MODEL), lambda i, j: (jnp.minimum(i, nbp - 1), 0)),
            pl.BlockSpec((tm, D_MODEL), lambda i, j: (jnp.maximum(i - nbp, 0), 0)),
            pl.BlockSpec((1, D_MODEL), lambda i, j: (0, 0)),
            pl.BlockSpec((D_MODEL, tn), lambda i, j: (0, j)),
        ],
        out_specs=pl.BlockSpec((tn // LANE, tm, LANE), lambda i, j: (j, i, 0)),
        out_shape=jax.ShapeDtypeStruct((n_out // LANE, tp + ts, LANE), BF16),
        scratch_shapes=[pltpu.VMEM((tm, D_MODEL), BF16)],
        compiler_params=_cparams("parallel", "arbitrary"),
        name="inproj",
    )(xp, xs, norm_w, w_bf16)


LOG2E = 1.4426950408889634


def _attn_bias_table():
    blk = ATTN_BLOCK
    qi = np.arange(blk)[:, None]
    kc = np.arange(3 * blk)[None, :]
    dist = np.abs(kc - blk - qi)
    slopes = 2.0 ** (-8.0 * np.arange(1, ATTN_HEADS + 1) / ATTN_HEADS)
    table = np.empty((4, ATTN_HEADS, blk, 3 * blk), np.float32)
    for first in (0, 1):
        for last in (0, 1):
            ok = (dist <= WINDOW) & ((kc >= blk) | (first == 0)) & ((kc < 2 * blk) | (last == 0))
            for h in range(ATTN_HEADS):
                table[2 * first + last, h] = np.where(ok, -slopes[h] * LOG2E * dist, -np.inf)
    return table


def _attn_kernel(q_ref, kp_ref, kc_ref, kn_ref, vp_ref, vc_ref, vn_ref, sink_ref, tab_ref, o_ref):
    n = pl.program_id(1)
    nb = pl.num_programs(1)
    edge = 2 * (n == 0).astype(jnp.int32) + (n == nb - 1).astype(jnp.int32)
    scale2 = HEAD_DIM ** -0.5 * LOG2E
    for g in range(ATTN_KV_HEADS):
        heads = range(g * ATTN_GROUP, (g + 1) * ATTN_GROUP)
        kcat = jnp.concatenate([kp_ref[g], kc_ref[g], kn_ref[g]], axis=0)
        vcat = jnp.concatenate([vp_ref[g], vc_ref[g], vn_ref[g]], axis=0)
        logits = [lax.dot_general(q_ref[h], kcat, (((1,), (1,)), ((), ())), preferred_element_type=F32)
                  * scale2 + tab_ref[edge, h] for h in heads]
        probs = []
        for h, s in zip(heads, logits):
            sink2 = sink_ref[0, h] * LOG2E
            m = jnp.maximum(jnp.max(s, axis=-1, keepdims=True), sink2)
            p = jnp.exp2(s - m)
            denom = jnp.sum(p, axis=-1, keepdims=True) + jnp.exp2(sink2 - m)
            probs.append((p.astype(BF16), 1.0 / denom))
        for h, (p, inv) in zip(heads, probs):
            o = jnp.dot(p, vcat, preferred_element_type=F32) * inv
            o_ref[:, h * HEAD_DIM:(h + 1) * HEAD_DIM] = o.astype(BF16)


def _attention(proj, sink, n_seq, seq):
    nb = seq // ATTN_BLOCK
    t = n_seq * seq
    table = jnp.asarray(_attn_bias_table())

    def kv_spec(slab_pair, shift):
        def imap(b, n):
            return (slab_pair, b * nb + jnp.clip(n + shift, 0, nb - 1), 0)
        return pl.BlockSpec((ATTN_KV_HEADS, ATTN_BLOCK, LANE), imap)

    return pl.pallas_call(
        _attn_kernel,
        grid=(n_seq, nb),
        in_specs=[
            pl.BlockSpec((ATTN_HEADS, ATTN_BLOCK, LANE), lambda b, n: (SLAB_QA // ATTN_HEADS, b * nb + n, 0)),
            kv_spec(SLAB_KA // ATTN_KV_HEADS, -1), kv_spec(SLAB_KA // ATTN_KV_HEADS, 0),
            kv_spec(SLAB_KA // ATTN_KV_HEADS, 1),
            kv_spec(SLAB_VA // ATTN_KV_HEADS, -1), kv_spec(SLAB_VA // ATTN_KV_HEADS, 0),
            kv_spec(SLAB_VA // ATTN_KV_HEADS, 1),
            pl.BlockSpec(memory_space=pltpu.SMEM),
            pl.BlockSpec(table.shape, lambda b, n: (0, 0, 0, 0)),
        ],
        out_specs=pl.BlockSpec((ATTN_BLOCK, ATTN_HEADS * HEAD_DIM), lambda b, n: (b * nb + n, 0)),
        out_shape=jax.ShapeDtypeStruct((t, ATTN_HEADS * HEAD_DIM), BF16),
        compiler_params=_cparams("parallel", "arbitrary"),
        name="attn",
    )(proj, proj, proj, proj, proj, proj, proj, sink, table)


def _hgrn_halves(c):
    halves = []
    h = c // 2
    while h >= 1:
        halves.append(h)
        h //= 2
    return halves


def _hgrn_constants(c, reverse):
    halves = _hgrn_halves(c)
    m = np.zeros((2, c, c), np.float32)
    level = -np.ones((c, c), np.int32)
    sign = np.zeros((sum(h >= HG_VPU_MIN_HALF for h in halves), c, LANE), np.float32)
    for t in range(c):
        m[0, t, :t + 1] = 1
        level[t, t] = 0
    for li, h in enumerate(halves):
        for t in range(c):
            pos = t % (2 * h)
            mid = t - pos + h
            if pos >= h:
                level[t, t - pos:mid] = li + 1
            if h == 2:
                if pos >= h:
                    m[1, t, mid:t + 1] = 1
                else:
                    m[1, t, t + 1:mid] = 1
            if h >= HG_VPU_MIN_HALF:
                sign[li, t, :] = 1.0 if pos >= h else -1.0
    if reverse:
        m = m[:, ::-1, ::-1]
        level = level[::-1, ::-1]
        sign = sign[:, ::-1, :]
    m = m.reshape(-1, c)
    return (np.ascontiguousarray(np.concatenate([m, m], axis=1)), np.ascontiguousarray(level),
            np.ascontiguousarray(sign))


def _sigmoid(x):
    return 0.5 + 0.5 * jnp.tanh(0.5 * x)


def _hgrn_kernel(c, q_ref, ff_ref, fb_ref, i_ref, g_ref, lb_ref, nw_ref,
                 mf_ref, mb_ref, lvf_ref, lvb_ref, sgf_ref, sgb_ref, o_ref, of_ref, ob_ref, qs_ref, st_ref):
    n_heads, seq = q_ref.shape[0], q_ref.shape[1]
    nc = seq // c
    halves = _hgrn_halves(c)
    row = lax.broadcasted_iota(jnp.int32, (c, LANE), 0)
    sub = 8

    lbr = lb_ref[...]
    lbe = jnp.exp(lbr - jnp.max(lbr, axis=0, keepdims=True))
    lb_all = lbe[0] / jnp.sum(lbe, axis=0)

    st_ref[...] = jnp.zeros_like(st_ref)

    def qs_body(it, carry):
        r0 = pl.multiple_of(it * c, c)
        for hd in range(n_heads):
            qp = q_ref[hd, pl.ds(r0, c), :].astype(F32)
            qs_ref[hd, pl.ds(r0, c), :] = qp * _sigmoid(qp)
        return carry

    lax.fori_loop(0, nc, qs_body, 0)

    def gates(ch):
        hd, r0, z_ref, m_ref, _, _, reverse = ch
        d = 1 if reverse else 0
        lb = lb_all[d:d + 1, hd * LANE:(hd + 1) * LANE]
        z = z_ref[hd, pl.ds(r0, c), :].astype(F32)
        sig = _sigmoid(z)
        f = lb + (1.0 - lb) * sig
        k = (1.0 - lb) * (1.0 - sig)
        g = jnp.log2(f)
        g_hi = g.astype(BF16)
        g_lo = (g - g_hi.astype(F32)).astype(BF16)
        e = jnp.dot(m_ref[...], jnp.concatenate([g_hi, g_lo], axis=0), preferred_element_type=F32)
        return dict(f=f, k=k, e=e, b=e[0:c, :], qs=qs_ref[hd, pl.ds(r0, c), :])

    def level(ch, s, li):
        _, _, _, _, lv_ref, sg_ref, reverse = ch
        qs, k, b = s["qs"], s["k"], s["b"]
        if li < 0:
            x_q, x_k = qs.astype(BF16), k.astype(BF16)
            h = 1
        else:
            h = halves[li]
            if h == 1:
                w = jnp.where((row & 1) == (0 if reverse else 1), s["f"], 1.0)
            elif h == 2:
                w = jnp.exp2(s["e"][c:2 * c, :])
            elif h < sub:
                b3 = b.reshape(c // (2 * h), 2 * h, LANE)
                mid = h if reverse else h - 1
                w = jnp.exp2((b3 - b3[:, mid:mid + 1, :]).reshape(c, LANE) * sg_ref[li])
            else:
                parts = []
                for j in range(c // h):
                    mid = (j // 2) * 2 * h + (h if reverse else h - 1)
                    blk, ref_row = b[j * h:(j + 1) * h], b[mid:mid + 1]
                    parts.append(blk - ref_row if (j % 2 == 1) != reverse else ref_row - blk)
                w = jnp.exp2(jnp.concatenate(parts, axis=0))
            if h >= sub:
                base = jnp.concatenate(
                    [(qs if (j % 2 == 1) != reverse else k)[j * h:(j + 1) * h] for j in range(c // h)], axis=0)
            else:
                base = jnp.where((row & h) == (0 if reverse else h), qs, k)
            x_q = x_k = (base * w).astype(BF16)
        p = lax.dot_general(x_q, x_k, (((1,), (1,)), ((), ())), preferred_element_type=F32)
        a = s.get("a")
        new_a = []
        for j in range(c // sub):
            rows = slice(j * sub, (j + 1) * sub)
            if a is None:
                new_a.append(jnp.where(lv_ref[rows, :] == 0, p[rows], 0.0))
            elif h < sub or ((j * sub // h) % 2 == 1) != reverse:
                new_a.append(jnp.where(lv_ref[rows, :] == li + 1, p[rows], a[j]))
            else:
                new_a.append(a[j])
        s["a"] = new_a

    def apply(ch, s):
        hd, r0, _, _, _, _, reverse = ch
        d = 1 if reverse else 0
        qs, k, b = s["qs"], s["k"], s["b"]
        v = i_ref[hd, pl.ds(r0, c), :]
        last = 0 if reverse else c - 1
        b_last = b[last:last + 1, :]
        st = st_ref[hd, d]
        o = jnp.dot(jnp.concatenate(s["a"], axis=0).astype(BF16), v, preferred_element_type=F32)
        o = o + lax.dot_general((qs * jnp.exp2(b)).astype(BF16), st.astype(BF16), (((1,), (1,)), ((), ())),
                                preferred_element_type=F32)
        k_out = (k * jnp.exp2(b_last - b)).astype(BF16)
        st_ref[hd, d] = st * jnp.exp2(b_last) + lax.dot_general(
            v, k_out, (((0,), (0,)), ((), ())), preferred_element_type=F32)
        return o

    def scan_body(it, carry):
        rf = pl.multiple_of(it * c, c)
        rb = pl.multiple_of((nc - 1 - it) * c, c)
        chains = []
        for hd in range(n_heads):
            chains.append((hd, rf, ff_ref, mf_ref, lvf_ref, sgf_ref, False))
            chains.append((hd, rb, fb_ref, mb_ref, lvb_ref, sgb_ref, True))
        states = [gates(ch) for ch in chains]
        for li in range(-1, len(halves)):
            for ch, s in zip(chains, states):
                level(ch, s, li)
        for ch, s in zip(chains, states):
            out_ref = ob_ref if ch[6] else of_ref
            out_ref[ch[0], pl.ds(ch[1], c), :] = apply(ch, s)
        return carry

    lax.fori_loop(0, nc, scan_body, 0)

    def out_body(it, carry):
        r0 = pl.multiple_of(it * c, c)
        for hd in range(n_heads):
            tot = of_ref[hd, pl.ds(r0, c), :] + ob_ref[hd, pl.ds(r0, c), :]
            y = tot * lax.rsqrt(jnp.mean(tot * tot, axis=-1, keepdims=True) + EPS) * nw_ref[...]
            gp = g_ref[hd, pl.ds(r0, c), :].astype(F32)
            o_ref[pl.ds(r0, c), hd * LANE:(hd + 1) * LANE] = (y * (gp * _sigmoid(gp))).astype(BF16)
        return carry

    lax.fori_loop(0, nc, out_body, 0)


def _hgrn(proj, hg_lb, norm_w, n_seq, seq):
    c = HG_CHUNK
    hh = HG_HEADS_PER_STEP
    mf, lvf, sgf = _hgrn_constants(c, False)
    mb, lvb, sgb = _hgrn_constants(c, True)
    t = n_seq * seq

    def slab(first):
        return pl.BlockSpec((hh, seq, LANE), lambda b, h: (first // hh + h, b, 0))

    def const(arr):
        return pl.BlockSpec(arr.shape, lambda b, h: (0,) * arr.ndim)

    return pl.pallas_call(
        functools.partial(_hgrn_kernel, c),
        grid=(n_seq, HG_HEADS // hh),
        in_specs=[
            slab(SLAB_QH), slab(SLAB_FF), slab(SLAB_FB), slab(SLAB_IH), slab(SLAB_GH),
            pl.BlockSpec((hg_lb.shape[0], 2, hh * LANE), lambda b, h: (0, 0, h)),
            pl.BlockSpec((1, LANE), lambda b, h: (0, 0)),
            const(mf), const(mb), const(lvf), const(lvb), const(sgf), const(sgb),
        ],
        out_specs=pl.BlockSpec((seq, hh * LANE), lambda b, h: (b, h)),
        out_shape=jax.ShapeDtypeStruct((t, HG_HEADS * LANE), BF16),
        scratch_shapes=[pltpu.VMEM((hh, seq, LANE), F32)] * 3 + [pltpu.VMEM((hh, 2, LANE, LANE), F32)],
        compiler_params=_cparams("parallel", "arbitrary"),
        name="hgrn",
    )(proj, proj, proj, proj, proj, hg_lb, norm_w,
      jnp.asarray(mf, BF16), jnp.asarray(mb, BF16), jnp.asarray(lvf), jnp.asarray(lvb),
      jnp.asarray(sgf), jnp.asarray(sgb))


def _outproj_kernel(nbp, xp_ref, xs_ref, a_ref, hg_ref, wo_ref, n2_ref, wrh_ref, wrl_ref, br_ref, tri_ref,
                    x1_ref, h2_ref, ri_ref, rw_ref, cnt_ref, run_ref):
    i = pl.program_id(0)
    half = a_ref.shape[1]
    sub_rows = tri_ref.shape[0]
    n_sub = a_ref.shape[0] // sub_rows

    @pl.when(i == 0)
    def _():
        run_ref[...] = jnp.zeros_like(run_ref)

    groups = [slice(j * sub_rows, (j + 1) * sub_rows) for j in range(n_sub)]
    from_prompt = i < nbp

    x1s = []
    for rows in groups:
        acc = jnp.dot(a_ref[rows, :], wo_ref[0:half, :], preferred_element_type=F32)
        acc = acc + jnp.dot(hg_ref[rows, :], wo_ref[half:2 * half, :], preferred_element_type=F32)
        x1 = jnp.where(from_prompt, xp_ref[rows, :], xs_ref[rows, :]) + acc
        x1_ref[rows, :] = x1
        x1s.append(x1)

    logits_all = []
    for rows, x1 in zip(groups, x1s):
        h2 = x1 * lax.rsqrt(jnp.mean(x1 * x1, axis=-1, keepdims=True) + EPS) * n2_ref[...]
        for j in range(ROW_PLANES):
            h2_ref[pl.ds(rows.start * ROW_PLANES + j, sub_rows, stride=ROW_PLANES), :] = (
                h2[:, j * LANE:(j + 1) * LANE])
        h_hi = h2.astype(BF16)
        h_lo = (h2 - h_hi.astype(F32)).astype(BF16)
        logits_all.append((jnp.dot(h_hi, wrh_ref[...], preferred_element_type=F32)
                           + jnp.dot(h_lo, wrh_ref[...], preferred_element_type=F32)
                           + jnp.dot(h_hi, wrl_ref[...], preferred_element_type=F32)) + br_ref[...])

    lane = lax.broadcasted_iota(jnp.int32, logits_all[0].shape, 1).astype(F32)
    far = float(LANE)
    is_g = lane < N_GROUPS
    picks = []
    for logits in logits_all:
        gl = jnp.where(is_g, logits, -jnp.inf)
        gmax = jnp.max(gl, axis=-1, keepdims=True)
        gidx = jnp.min(jnp.where(gl == gmax, lane, far), axis=-1, keepdims=True)
        g_w = 1.0 / jnp.sum(jnp.where(is_g, jnp.exp(logits - gmax), 0.0), axis=-1, keepdims=True)
        lo = N_GROUPS + gidx * EXPERTS_PER_GROUP
        in_grp = (lane >= lo) & (lane < lo + EXPERTS_PER_GROUP)
        el = jnp.where(in_grp, logits, -jnp.inf)
        m1 = jnp.max(el, axis=-1, keepdims=True)
        i1 = jnp.min(jnp.where(el == m1, lane, far), axis=-1, keepdims=True)
        el2 = jnp.where(lane == i1, -jnp.inf, el)
        m2 = jnp.max(el2, axis=-1, keepdims=True)
        i2 = jnp.min(jnp.where(el2 == m2, lane, far), axis=-1, keepdims=True)
        p2 = jnp.exp(m2 - m1)
        picks.append((i1, i2, g_w * (1.0 / (1.0 + p2)), g_w * (p2 / (1.0 + p2))))

    run = run_ref[...]
    for rows, (i1, i2, c1, c2) in zip(groups, picks):
        oh1 = lane == i1
        oh2 = lane == i2
        one1 = jnp.where(oh1, 1.0, 0.0)
        one2 = jnp.where(oh2, 1.0, 0.0)
        before1 = jnp.dot(tri_ref[...], one1.astype(BF16), preferred_element_type=F32)
        before2 = jnp.dot(tri_ref[...], one2.astype(BF16), preferred_element_type=F32)
        tot1 = jnp.sum(one1, axis=0, keepdims=True)
        r1 = jnp.sum(jnp.where(oh1, run + before1, 0.0), axis=-1, keepdims=True)
        r2 = jnp.sum(jnp.where(oh2, run + tot1 + before2, 0.0), axis=-1, keepdims=True)
        run = run + tot1 + jnp.sum(one2, axis=0, keepdims=True)
        ri = jnp.where(lane == 0.0, i1 - N_GROUPS, jnp.where(lane == 1.0, i2 - N_GROUPS,
                       jnp.where(lane == 2.0, r1, jnp.where(lane == 3.0, r2, 0.0))))
        ri_ref[rows, :] = ri.astype(jnp.int32)
        rw_ref[rows, :] = jnp.where(lane == 0.0, c1, jnp.where(lane == 1.0, c2, 0.0))
    run_ref[...] = run
    cnt_ref[...] = run.astype(jnp.int32)


def _outproj(xp, xs, attn, hg, wo_bf16, norm2_w, wr_hi, wr_lo, br):
    tp, ts = xp.shape[0], xs.shape[0]
    t = tp + ts
    tm = _largest_divisor(np.gcd(tp, ts), ROUTE_TILE)
    nbp, nbs = tp // tm, ts // tm
    half = attn.shape[1]
    sub_rows = _largest_divisor(tm, ROUTE_SUB)
    strict_lower = jnp.asarray(np.tril(np.ones((sub_rows, sub_rows), np.float32), -1), BF16)

    def const(shape):
        return pl.BlockSpec(shape, lambda i: (0, 0))

    def rows(width):
        return pl.BlockSpec((tm, width), lambda i: (i, 0))

    return pl.pallas_call(
        functools.partial(_outproj_kernel, nbp),
        grid=(nbp + nbs,),
        in_specs=[
            pl.BlockSpec((tm, D_MODEL), lambda i: (jnp.minimum(i, nbp - 1), 0)),
            pl.BlockSpec((tm, D_MODEL), lambda i: (jnp.maximum(i - nbp, 0), 0)),
            rows(half), rows(half),
            const((2 * half, D_MODEL)), const((1, D_MODEL)),
            const((D_MODEL, LANE)), const((D_MODEL, LANE)), const((1, LANE)), const((sub_rows, sub_rows)),
        ],
        out_specs=[rows(D_MODEL), pl.BlockSpec((tm * ROW_PLANES, LANE), lambda i: (i, 0)),
                   rows(LANE), rows(LANE), const((1, LANE))],
        out_shape=[
            jax.ShapeDtypeStruct((t, D_MODEL), F32),
            jax.ShapeDtypeStruct((t * ROW_PLANES, LANE), F32),
            jax.ShapeDtypeStruct((t, LANE), jnp.int32),
            jax.ShapeDtypeStruct((t, LANE), F32),
            jax.ShapeDtypeStruct((1, LANE), jnp.int32),
        ],
        scratch_shapes=[pltpu.VMEM((1, LANE), F32)],
        compiler_params=_cparams("arbitrary"),
        name="outproj",
    )(xp, xs, attn, hg, wo_bf16, norm2_w, wr_hi, wr_lo, br, strict_lower)


def _plan_expert_tiles(ri, counts, tm):
    t = ri.shape[0]
    n_slots = 2 * t
    n_tiles = n_slots // tm + N_EXPERTS
    p = n_tiles * tm
    ef = jnp.concatenate([ri[:, 0], ri[:, 1]])
    rank = jnp.concatenate([ri[:, 2], ri[:, 3]])
    tiles_per = (counts + tm - 1) // tm
    tile_end = jnp.cumsum(tiles_per)
    seg_start = (tile_end - tiles_per) * tm
    experts = jnp.arange(N_EXPERTS, dtype=jnp.int32)
    ppos = jnp.sum(jnp.where(ef[:, None] == experts[None, :], seg_start[None, :], 0), axis=1) + rank
    slot = jnp.arange(n_slots, dtype=jnp.int32)
    real_dest = jnp.full((p,), -1, jnp.int32).at[ppos].set(slot, unique_indices=True)
    is_pad = real_dest < 0
    pos = jnp.arange(p, dtype=jnp.int32)
    dest = jnp.where(is_pad, n_slots + pos % (2 * tm), real_dest)
    src_tok = jnp.where(is_pad, 0, real_dest % t)
    tiles = jnp.arange(n_tiles, dtype=jnp.int32)
    tile_expert = jnp.minimum(jnp.sum((tile_end[None, :] <= tiles[:, None]).astype(jnp.int32), axis=1),
                              N_EXPERTS - 1)
    n_used = tile_end[N_EXPERTS - 1:].astype(jnp.int32)
    return src_tok.reshape(n_tiles, 1, tm), dest.reshape(n_tiles, 1, tm), tile_expert, n_used, n_tiles


def _experts_kernel(te_ref, nu_ref, src_ref, srcn_ref, dstp_ref, dst_ref, h2_hbm, wg_ref, wu_ref, wd_ref, y_hbm,
                    gbuf, obuf, wg_bf, wu_bf, wd_bf, gsem, ssem):
    i = pl.program_id(0)
    n = nu_ref[0]
    pitch = EXPERT_ROW_PITCH
    n_planes = ROW_PLANES
    tm = gbuf.shape[1] // pitch
    slot = i % 2
    other = 1 - slot
    live = i < n

    def gather_row(idx_ref, k, s):
        src = h2_hbm.at[pl.ds(pl.multiple_of(idx_ref[0, 0, k], n_planes), n_planes), :]
        return pltpu.make_async_copy(src, gbuf.at[s, pl.ds(k * pitch, n_planes), :], gsem.at[s])

    def scatter_row(idx_ref, k, s):
        dst = y_hbm.at[pl.ds(pl.multiple_of(idx_ref[0, 0, k], n_planes), n_planes), :]
        return pltpu.make_async_copy(obuf.at[s, pl.ds(k * pitch, n_planes), :], dst, ssem.at[s])

    def wait_gather(s):
        pltpu.make_async_copy(h2_hbm.at[pl.ds(0, tm * n_planes), :], gbuf.at[s, pl.ds(0, tm * n_planes), :],
                              gsem.at[s]).wait()

    def wait_scatter(s):
        pltpu.make_async_copy(obuf.at[s, pl.ds(0, tm * n_planes), :], y_hbm.at[pl.ds(0, tm * n_planes), :],
                              ssem.at[s]).wait()

    @pl.when(i == 0)
    def _():
        def body(k, carry):
            gather_row(src_ref, k, 0).start()
            return carry
        lax.fori_loop(0, tm, body, 0, unroll=8)
        obuf[1] = jnp.zeros(obuf.shape[1:], obuf.dtype)
        spare0 = y_hbm.shape[0] - 2 * tm * n_planes
        for part in range(2):
            fill = pltpu.make_async_copy(
                obuf.at[1, pl.ds(0, tm * n_planes), :],
                y_hbm.at[pl.ds(spare0 + part * tm * n_planes, tm * n_planes), :], ssem.at[0])
            fill.start()
            fill.wait()

    @pl.when((i >= 1) & live)
    def _():
        wait_scatter(slot)

    changed = live & ((i == 0) | (te_ref[i] != te_ref[jnp.maximum(i - 1, 0)]))

    @pl.when(changed)
    def _():
        wg_bf[...] = wg_ref[0].astype(BF16)
        wu_bf[...] = wu_ref[0].astype(BF16)
        wd_bf[...] = wd_ref[0].astype(BF16)

    def tile_body(s):
        wait_gather(s)
        hb = jnp.concatenate([gbuf[s, pl.ds(j, tm, stride=pitch), :].astype(BF16) for j in range(n_planes)],
                             axis=1)
        for k in range(tm):
            gather_row(srcn_ref, k, 1 - s).start()
        for k in range(tm):
            scatter_row(dstp_ref, k, 1 - s).start()
        gate = jnp.dot(hb, wg_bf[...], preferred_element_type=F32)
        up = jnp.dot(hb, wu_bf[...], preferred_element_type=F32)
        act = (gate * _sigmoid(gate) * up).astype(BF16)
        y = jnp.dot(act, wd_bf[...], preferred_element_type=F32)
        for j in range(n_planes):
            obuf[s, pl.ds(j, tm, stride=pitch), :] = y[:, j * LANE:(j + 1) * LANE]

    @pl.when((slot == 0) & live)
    def _():
        tile_body(0)

    @pl.when((slot == 1) & live)
    def _():
        tile_body(1)

    @pl.when(i == n - 1)
    def _():
        wait_gather(other)
        wait_scatter(other)

        def body(k, carry):
            scatter_row(dst_ref, k, slot).start()
            return carry
        lax.fori_loop(0, tm, body, 0, unroll=8)
        wait_scatter(slot)


def _experts(h2, src_tok, dest, tile_expert, n_used, n_tiles, w_gate, w_up, w_down):
    tm = EXPERT_TILE
    n_planes = ROW_PLANES
    t = h2.shape[0] // n_planes
    src_tok, dest = src_tok * n_planes, dest * n_planes

    def idx_spec(shift):
        return pl.BlockSpec((1, 1, tm), lambda i, te, nu: (jnp.minimum(i + shift, nu[0] - 1), 0, 0),
                            memory_space=pltpu.SMEM)

    def w_spec(shape):
        return pl.BlockSpec((1,) + shape, lambda i, te, nu: (te[jnp.minimum(i, nu[0] - 1)], 0, 0))

    spare = ((2 * t + jnp.arange(tm, dtype=jnp.int32)) * n_planes).reshape(1, 1, tm)
    dest_prev = jnp.concatenate([spare, dest[:-1]], axis=0)

    grid_spec = pltpu.PrefetchScalarGridSpec(
        num_scalar_prefetch=2,
        grid=(n_tiles,),
        in_specs=[
            idx_spec(0), idx_spec(1), idx_spec(0), idx_spec(0),
            pl.BlockSpec(memory_space=pl.ANY),
            w_spec((D_MODEL, D_EXPERT)), w_spec((D_MODEL, D_EXPERT)), w_spec((D_EXPERT, D_MODEL)),
        ],
        out_specs=pl.BlockSpec(memory_space=pl.ANY),
        scratch_shapes=[
            pltpu.VMEM((2, tm * EXPERT_ROW_PITCH, LANE), F32),
            pltpu.VMEM((2, tm * EXPERT_ROW_PITCH, LANE), F32),
            pltpu.VMEM((D_MODEL, D_EXPERT), BF16),
            pltpu.VMEM((D_MODEL, D_EXPERT), BF16),
            pltpu.VMEM((D_EXPERT, D_MODEL), BF16),
            pltpu.SemaphoreType.DMA((2,)),
            pltpu.SemaphoreType.DMA((2,)),
        ],
    )
    return pl.pallas_call(
        _experts_kernel,
        grid_spec=grid_spec,
        out_shape=jax.ShapeDtypeStruct(((2 * t + 2 * tm) * n_planes, LANE), F32),
        compiler_params=_cparams("arbitrary"),
        name="experts",
    )(tile_expert, n_used, src_tok, src_tok, dest_prev, dest, h2, w_gate, w_up, w_down)


def _final_kernel(x1_ref, ya_ref, yb_ref, rw_ref, nw_ref, o_ref):
    rw = rw_ref[...]
    c1, c2 = rw[:, 0:1], rw[:, 1:2]
    tm = x1_ref.shape[0]
    planes = []
    for j in range(ROW_PLANES):
        cols = slice(j * LANE, (j + 1) * LANE)
        rows = pl.ds(j, tm, stride=ROW_PLANES)
        planes.append(x1_ref[:, cols] + c1 * ya_ref[rows, :] + c2 * yb_ref[rows, :])
    sq = planes[0] * planes[0]
    for xj in planes[1:]:
        sq = sq + xj * xj
    r = lax.rsqrt(jnp.sum(sq, axis=-1, keepdims=True) * (1.0 / o_ref.shape[1]) + EPS)
    for j, xj in enumerate(planes):
        cols = slice(j * LANE, (j + 1) * LANE)
        o_ref[:, cols] = xj * r * nw_ref[:, cols]


def _final(x1, y, rw, norm_w, row0, n_rows):
    t = x1.shape[0]
    tm = _largest_divisor(np.gcd(np.gcd(row0, n_rows), t), 512)
    b0 = row0 // tm
    bt = t // tm

    return pl.pallas_call(
        _final_kernel,
        grid=(n_rows // tm,),
        in_specs=[
            pl.BlockSpec((tm, D_MODEL), lambda i: (b0 + i, 0)),
            pl.BlockSpec((tm * ROW_PLANES, LANE), lambda i: (b0 + i, 0)),
            pl.BlockSpec((tm * ROW_PLANES, LANE), lambda i: (bt + b0 + i, 0)),
            pl.BlockSpec((tm, LANE), lambda i: (b0 + i, 0)),
            pl.BlockSpec((1, D_MODEL), lambda i: (0, 0)),
        ],
        out_specs=pl.BlockSpec((tm, D_MODEL), lambda i: (i, 0)),
        out_shape=jax.ShapeDtypeStruct((n_rows, D_MODEL), F32),
        compiler_params=_cparams("parallel"),
        name="final",
    )(x1, y, y, rw, norm_w)


def kernel(x_prompt, x_sample, norm1_w, w_in, attn_sink, hg_lb, hg_norm_w, w_out, norm2_w, w_router_g,
           b_router_g, w_router_e, b_router_e, w_gate, w_up, w_down, final_norm_w):
    bp, seq, d = x_prompt.shape
    bs, seq_s, _ = x_sample.shape
    assert d == D_MODEL and seq == seq_s and seq % max(ATTN_BLOCK, HG_CHUNK) == 0
    assert w_in.shape[0] == 1 and w_in.shape[2] == N_SLABS * LANE
    n_seq = bp + bs
    tp, ts = bp * seq, bs * seq
    xp = x_prompt.reshape(tp, d)
    xs = x_sample.reshape(ts, d)

    proj = _inproj(xp, xs, norm1_w, w_in[0].astype(BF16))
    attn = _attention(proj, attn_sink, n_seq, seq)
    hg = _hgrn(proj, hg_lb, hg_norm_w, n_seq, seq)

    wr = jnp.concatenate([w_router_g[0], w_router_e[0].reshape(d, N_EXPERTS)], axis=1)
    wr = jnp.pad(wr, ((0, 0), (0, LANE - wr.shape[1])))
    wr_hi = wr.astype(BF16)
    wr_lo = (wr - wr_hi.astype(F32)).astype(BF16)
    br = jnp.concatenate([b_router_g[0], b_router_e[0].reshape(N_EXPERTS)])
    br = jnp.pad(br, (0, LANE - br.shape[0])).reshape(1, LANE)
    x1, h2, ri, rw, cnt = _outproj(xp, xs, attn, hg, w_out[0].astype(BF16), norm2_w, wr_hi, wr_lo, br)

    counts = cnt[0, N_GROUPS:N_GROUPS + N_EXPERTS]
    src_tok, dest, tile_expert, n_used, n_tiles = _plan_expert_tiles(ri, counts, EXPERT_TILE)
    y = _experts(h2, src_tok, dest, tile_expert, n_used, n_tiles, w_gate[0], w_up[0], w_down[0])

    fnw = final_norm_w.reshape(1, d)
    y_prompt = _final(x1, y, rw, fnw, 0, tp).reshape(bp, seq, d)
    y_sample = _final(x1, y, rw, fnw, tp, ts).reshape(bs, seq, d)
    return (y_prompt, y_sample)
```

```python
import functools

import numpy as np
import jax
import jax.numpy as jnp
from jax import lax
from jax.experimental import pallas as pl
from jax.experimental.pallas import tpu as pltpu

F32 = jnp.float32
BF16 = jnp.bfloat16
U32 = jnp.uint32

LANE = 128
D_MODEL = 2048
ROW_PLANES = D_MODEL // (2 * LANE)
ATTN_HEADS = 8
ATTN_KV_HEADS = 2
ATTN_GROUP = ATTN_HEADS // ATTN_KV_HEADS
HEAD_DIM = LANE
WINDOW = 128
ATTN_BLOCK = 128
HG_HEADS = 8
HG_CHUNK = 128
HG_VPU_MIN_HALF = 4
HG_HEADS_PER_STEP = 4
N_GROUPS = 4
EXPERTS_PER_GROUP = 8
N_EXPERTS = N_GROUPS * EXPERTS_PER_GROUP
D_EXPERT = 512
EPS = 1e-6

SLAB_QA, SLAB_KA, SLAB_VA, SLAB_QH, SLAB_FF, SLAB_FB, SLAB_IH, SLAB_GH = 0, 8, 10, 12, 20, 28, 36, 44
N_SLABS = 52

EXPERT_TILE = 256
EXPERT_ROW_PITCH = ROW_PLANES
ROUTE_TILE = 512
ROUTE_SUB = 128
VMEM_LIMIT = 56 * 1024 * 1024


def _pack_row(x):
    half = D_MODEL // 2
    return [pltpu.pack_elementwise([x[:, i * LANE:(i + 1) * LANE], x[:, half + i * LANE:half + (i + 1) * LANE]],
                                   packed_dtype=BF16) for i in range(ROW_PLANES)]


def _unpack_row(planes):
    return [pltpu.unpack_elementwise(w, index=part, packed_dtype=BF16, unpacked_dtype=F32)
            for part in (0, 1) for w in planes]


def _largest_divisor(n, cap):
    n = int(n)
    d = min(n, cap)
    while n % d:
        d -= 1
    return d


def _cparams(*sem):
    return pltpu.CompilerParams(dimension_semantics=sem, vmem_limit_bytes=VMEM_LIMIT)


def _inproj_kernel(nbp, xp_ref, xs_ref, nw_ref, w_ref, o_ref, h_ref):
    i = pl.program_id(0)
    j = pl.program_id(1)

    def norm_into_scratch(x_ref):
        x = x_ref[...]
        ms = jnp.mean(x * x, axis=-1, keepdims=True)
        h_ref[...] = (x * lax.rsqrt(ms + EPS) * nw_ref[...]).astype(BF16)

    @pl.when((j == 0) & (i < nbp))
    def _():
        norm_into_scratch(xp_ref)

    @pl.when((j == 0) & (i >= nbp))
    def _():
        norm_into_scratch(xs_ref)

    acc = jnp.dot(h_ref[...], w_ref[...], preferred_element_type=F32)
    for c in range(o_ref.shape[0]):
        o_ref[c] = acc[:, c * LANE:(c + 1) * LANE].astype(BF16)


def _inproj(xp, xs, norm_w, w_bf16):
    tp, ts = xp.shape[0], xs.shape[0]
    tm = _largest_divisor(np.gcd(tp, ts), 1024)
    tn = 512
    nbp, nbs = tp // tm, ts // tm
    n_out = w_bf16.shape[1]
    return pl.pallas_call(
        functools.partial(_inproj_kernel, nbp),
        grid=(nbp + nbs, n_out // tn),
        in_specs=[
            pl.BlockSpec((tm, D_MODEL), lambda i, j: (jnp.minimum(i, nbp - 1), 0)),
            pl.BlockSpec((tm, D_MODEL), lambda i, j: (jnp.maximum(i - nbp, 0), 0)),
            pl.BlockSpec((1, D_MODEL), lambda i, j: (0, 0)),
            pl.BlockSpec((D_MODEL, tn), lambda i, j: (0, j)),
        ],
        out_specs=pl.BlockSpec((tn // LANE, tm, LANE), lambda i, j: (j, i, 0)),
        out_shape=jax.ShapeDtypeStruct((n_out // LANE, tp + ts, LANE), BF16),
        scratch_shapes=[pltpu.VMEM((tm, D_MODEL), BF16)],
        compiler_params=_cparams("parallel", "arbitrary"),
        name="inproj",
    )(xp, xs, norm_w, w_bf16)


LOG2E = 1.4426950408889634


def _attn_bias_table():
    blk = ATTN_BLOCK
    qi = np.arange(blk)[:, None]
    kc = np.arange(3 * blk)[None, :]
    dist = np.abs(kc - blk - qi)
    slopes = 2.0 ** (-8.0 * np.arange(1, ATTN_HEADS + 1) / ATTN_HEADS)
    table = np.empty((4, ATTN_HEADS, blk, 3 * blk), np.float32)
    for first in (0, 1):
        for last in (0, 1):
            ok = (dist <= WINDOW) & ((kc >= blk) | (first == 0)) & ((kc < 2 * blk) | (last == 0))
            for h in range(ATTN_HEADS):
                table[2 * first + last, h] = np.where(ok, -slopes[h] * LOG2E * dist, -np.inf)
    return table


def _attn_kernel(q_ref, kp_ref, kc_ref, kn_ref, vp_ref, vc_ref, vn_ref, sink_ref, tab_ref, o_ref):
    n = pl.program_id(1)
    nb = pl.num_programs(1)
    edge = 2 * (n == 0).astype(jnp.int32) + (n == nb - 1).astype(jnp.int32)
    scale2 = HEAD_DIM ** -0.5 * LOG2E
    for g in range(ATTN_KV_HEADS):
        heads = range(g * ATTN_GROUP, (g + 1) * ATTN_GROUP)
        kcat = jnp.concatenate([kp_ref[g], kc_ref[g], kn_ref[g]], axis=0)
        vcat = jnp.concatenate([vp_ref[g], vc_ref[g], vn_ref[g]], axis=0)
        logits = [lax.dot_general(q_ref[h], kcat, (((1,), (1,)), ((), ())), preferred_element_type=F32)
                  * scale2 + tab_ref[edge, h] for h in heads]
        probs = []
        for h, s in zip(heads, logits):
            sink2 = sink_ref[0, h] * LOG2E
            m = jnp.maximum(jnp.max(s, axis=-1, keepdims=True), sink2)
            p = jnp.exp2(s - m)
            denom = jnp.sum(p, axis=-1, keepdims=True) + jnp.exp2(sink2 - m)
            probs.append((p.astype(BF16), 1.0 / denom))
        for h, (p, inv) in zip(heads, probs):
            o = jnp.dot(p, vcat, preferred_element_type=F32) * inv
            o_ref[:, h * HEAD_DIM:(h + 1) * HEAD_DIM] = o.astype(BF16)


def _attention(proj, sink, n_seq, seq):
    nb = seq // ATTN_BLOCK
    t = n_seq * seq
    table = jnp.asarray(_attn_bias_table())

    def kv_spec(slab_pair, shift):
        def imap(b, n):
            return (slab_pair, b * nb + jnp.clip(n + shift, 0, nb - 1), 0)
        return pl.BlockSpec((ATTN_KV_HEADS, ATTN_BLOCK, LANE), imap)

    return pl.pallas_call(
        _attn_kernel,
        grid=(n_seq, nb),
        in_specs=[
            pl.BlockSpec((ATTN_HEADS, ATTN_BLOCK, LANE), lambda b, n: (SLAB_QA // ATTN_HEADS, b * nb + n, 0)),
            kv_spec(SLAB_KA // ATTN_KV_HEADS, -1), kv_spec(SLAB_KA // ATTN_KV_HEADS, 0),
            kv_spec(SLAB_KA // ATTN_KV_HEADS, 1),
            kv_spec(SLAB_VA // ATTN_KV_HEADS, -1), kv_spec(SLAB_VA // ATTN_KV_HEADS, 0),
            kv_spec(SLAB_VA // ATTN_KV_HEADS, 1),
            pl.BlockSpec(memory_space=pltpu.SMEM),
            pl.BlockSpec(table.shape, lambda b, n: (0, 0, 0, 0)),
        ],
        out_specs=pl.BlockSpec((ATTN_BLOCK, ATTN_HEADS * HEAD_DIM), lambda b, n: (b * nb + n, 0)),
        out_shape=jax.ShapeDtypeStruct((t, ATTN_HEADS * HEAD_DIM), BF16),
        compiler_params=_cparams("parallel", "arbitrary"),
        name="attn",
    )(proj, proj, proj, proj, proj, proj, proj, sink, table)


def _hgrn_halves(c):
    halves = []
    h = c // 2
    while h >= 1:
        halves.append(h)
        h //= 2
    return halves


def _hgrn_constants(c, reverse):
    halves = _hgrn_halves(c)
    m = np.zeros((2, c, c), np.float32)
    level = -np.ones((c, c), np.int32)
    sign = np.zeros((sum(h >= HG_VPU_MIN_HALF for h in halves), c, LANE), np.float32)
    for t in range(c):
        m[0, t, :t + 1] = 1
        level[t, t] = 0
    for li, h in enumerate(halves):
        for t in range(c):
            pos = t % (2 * h)
            mid = t - pos + h
            if pos >= h:
                level[t, t - pos:mid] = li + 1
            if h == 2:
                if pos >= h:
                    m[1, t, mid:t + 1] = 1
                else:
                    m[1, t, t + 1:mid] = 1
            if h >= HG_VPU_MIN_HALF:
                sign[li, t, :] = 1.0 if pos >= h else -1.0
    if reverse:
        m = m[:, ::-1, ::-1]
        level = level[::-1, ::-1]
        sign = sign[:, ::-1, :]
    m = m.reshape(-1, c)
    return (np.ascontiguousarray(np.concatenate([m, m], axis=1)), np.ascontiguousarray(level),
            np.ascontiguousarray(sign))


def _sigmoid(x):
    return 0.5 + 0.5 * jnp.tanh(0.5 * x)


def _hgrn_kernel(c, q_ref, ff_ref, fb_ref, i_ref, g_ref, lb_ref, nw_ref,
                 mf_ref, mb_ref, lvf_ref, lvb_ref, sgf_ref, sgb_ref, o_ref, of_ref, ob_ref, qs_ref, st_ref):
    n_heads, seq = q_ref.shape[0], q_ref.shape[1]
    nc = seq // c
    halves = _hgrn_halves(c)
    row = lax.broadcasted_iota(jnp.int32, (c, LANE), 0)
    sub = 8

    lbr = lb_ref[...]
    lbe = jnp.exp(lbr - jnp.max(lbr, axis=0, keepdims=True))
    lb_all = lbe[0] / jnp.sum(lbe, axis=0)

    st_ref[...] = jnp.zeros_like(st_ref)

    def qs_body(it, carry):
        r0 = pl.multiple_of(it * c, c)
        for hd in range(n_heads):
            qp = q_ref[hd, pl.ds(r0, c), :].astype(F32)
            qs_ref[hd, pl.ds(r0, c), :] = qp * _sigmoid(qp)
        return carry

    lax.fori_loop(0, nc, qs_body, 0)

    def gates(ch):
        hd, r0, z_ref, m_ref, _, _, reverse = ch
        d = 1 if reverse else 0
        lb = lb_all[d:d + 1, hd * LANE:(hd + 1) * LANE]
        z = z_ref[hd, pl.ds(r0, c), :].astype(F32)
        sig = _sigmoid(z)
        f = lb + (1.0 - lb) * sig
        k = (1.0 - lb) * (1.0 - sig)
        g = jnp.log2(f)
        g_hi = g.astype(BF16)
        g_lo = (g - g_hi.astype(F32)).astype(BF16)
        e = jnp.dot(m_ref[...], jnp.concatenate([g_hi, g_lo], axis=0), preferred_element_type=F32)
        return dict(f=f, k=k, e=e, b=e[0:c, :], qs=qs_ref[hd, pl.ds(r0, c), :])

    def level(ch, s, li):
        _, _, _, _, lv_ref, sg_ref, reverse = ch
        qs, k, b = s["qs"], s["k"], s["b"]
        if li < 0:
            x_q, x_k = qs.astype(BF16), k.astype(BF16)
            h = 1
        else:
            h = halves[li]
            if h == 1:
                w = jnp.where((row & 1) == (0 if reverse else 1), s["f"], 1.0)
            elif h == 2:
                w = jnp.exp2(s["e"][c:2 * c, :])
            elif h < sub:
                b3 = b.reshape(c // (2 * h), 2 * h, LANE)
                mid = h if reverse else h - 1
                w = jnp.exp2((b3 - b3[:, mid:mid + 1, :]).reshape(c, LANE) * sg_ref[li])
            else:
                parts = []
                for j in range(c // h):
                    mid = (j // 2) * 2 * h + (h if reverse else h - 1)
                    blk, ref_row = b[j * h:(j + 1) * h], b[mid:mid + 1]
                    parts.append(blk - ref_row if (j % 2 == 1) != reverse else ref_row - blk)
                w = jnp.exp2(jnp.concatenate(parts, axis=0))
            if h >= sub:
                base = jnp.concatenate(
                    [(qs if (j % 2 == 1) != reverse else k)[j * h:(j + 1) * h] for j in range(c // h)], axis=0)
            else:
                base = jnp.where((row & h) == (0 if reverse else h), qs, k)
            x_q = x_k = (base * w).astype(BF16)
        p = lax.dot_general(x_q, x_k, (((1,), (1,)), ((), ())), preferred_element_type=F32)
        a = s.get("a")
        new_a = []
        for j in range(c // sub):
            rows = slice(j * sub, (j + 1) * sub)
            if a is None:
                new_a.append(jnp.where(lv_ref[rows, :] == 0, p[rows], 0.0))
            elif h < sub or ((j * sub // h) % 2 == 1) != reverse:
                new_a.append(jnp.where(lv_ref[rows, :] == li + 1, p[rows], a[j]))
            else:
                new_a.append(a[j])
        s["a"] = new_a

    def apply(ch, s):
        hd, r0, _, _, _, _, reverse = ch
        d = 1 if reverse else 0
        qs, k, b = s["qs"], s["k"], s["b"]
        v = i_ref[hd, pl.ds(r0, c), :]
        last = 0 if reverse else c - 1
        b_last = b[last:last + 1, :]
        st = st_ref[hd, d]
        o = jnp.dot(jnp.concatenate(s["a"], axis=0).astype(BF16), v, preferred_element_type=F32)
        o = o + lax.dot_general((qs * jnp.exp2(b)).astype(BF16), st.astype(BF16), (((1,), (1,)), ((), ())),
                                preferred_element_type=F32)
        k_out = (k * jnp.exp2(b_last - b)).astype(BF16)
        st_ref[hd, d] = st * jnp.exp2(b_last) + lax.dot_general(
            v, k_out, (((0,), (0,)), ((), ())), preferred_element_type=F32)
        return o

    def scan_body(it, carry):
        rf = pl.multiple_of(it * c, c)
        rb = pl.multiple_of((nc - 1 - it) * c, c)
        chains = []
        for hd in range(n_heads):
            chains.append((hd, rf, ff_ref, mf_ref, lvf_ref, sgf_ref, False))
            chains.append((hd, rb, fb_ref, mb_ref, lvb_ref, sgb_ref, True))
        states = [gates(ch) for ch in chains]
        for li in range(-1, len(halves)):
            for ch, s in zip(chains, states):
                level(ch, s, li)
        for ch, s in zip(chains, states):
            out_ref = ob_ref if ch[6] else of_ref
            out_ref[ch[0], pl.ds(ch[1], c), :] = apply(ch, s)
        return carry

    lax.fori_loop(0, nc, scan_body, 0)

    def out_body(it, carry):
        r0 = pl.multiple_of(it * c, c)
        for hd in range(n_heads):
            tot = of_ref[hd, pl.ds(r0, c), :] + ob_ref[hd, pl.ds(r0, c), :]
            y = tot * lax.rsqrt(jnp.mean(tot * tot, axis=-1, keepdims=True) + EPS) * nw_ref[...]
            gp = g_ref[hd, pl.ds(r0, c), :].astype(F32)
            o_ref[pl.ds(r0, c), hd * LANE:(hd + 1) * LANE] = (y * (gp * _sigmoid(gp))).astype(BF16)
        return carry

    lax.fori_loop(0, nc, out_body, 0)


def _hgrn(proj, hg_lb, norm_w, n_seq, seq):
    c = HG_CHUNK
    hh = HG_HEADS_PER_STEP
    mf, lvf, sgf = _hgrn_constants(c, False)
    mb, lvb, sgb = _hgrn_constants(c, True)
    t = n_seq * seq

    def slab(first):
        return pl.BlockSpec((hh, seq, LANE), lambda b, h: (first // hh + h, b, 0))

    def const(arr):
        return pl.BlockSpec(arr.shape, lambda b, h: (0,) * arr.ndim)

    return pl.pallas_call(
        functools.partial(_hgrn_kernel, c),
        grid=(n_seq, HG_HEADS // hh),
        in_specs=[
            slab(SLAB_QH), slab(SLAB_FF), slab(SLAB_FB), slab(SLAB_IH), slab(SLAB_GH),
            pl.BlockSpec((hg_lb.shape[0], 2, hh * LANE), lambda b, h: (0, 0, h)),
            pl.BlockSpec((1, LANE), lambda b, h: (0, 0)),
            const(mf), const(mb), const(lvf), const(lvb), const(sgf), const(sgb),
        ],
        out_specs=pl.BlockSpec((seq, hh * LANE), lambda b, h: (b, h)),
        out_shape=jax.ShapeDtypeStruct((t, HG_HEADS * LANE), BF16),
        scratch_shapes=[pltpu.VMEM((hh, seq, LANE), F32)] * 3 + [pltpu.VMEM((hh, 2, LANE, LANE), F32)],
        compiler_params=_cparams("parallel", "arbitrary"),
        name="hgrn",
    )(proj, proj, proj, proj, proj, hg_lb, norm_w,
      jnp.asarray(mf, BF16), jnp.asarray(mb, BF16), jnp.asarray(lvf), jnp.asarray(lvb),
      jnp.asarray(sgf), jnp.asarray(sgb))


def _outproj_kernel(nbp, xp_ref, xs_ref, a_ref, hg_ref, wo_ref, n2_ref, wrh_ref, wrl_ref, br_ref, tri_ref,
                    x1_ref, h2_ref, ri_ref, rw_ref, cnt_ref, run_ref):
    i = pl.program_id(0)
    half = a_ref.shape[1]
    sub_rows = tri_ref.shape[0]
    n_sub = a_ref.shape[0] // sub_rows

    @pl.when(i == 0)
    def _():
        run_ref[...] = jnp.zeros_like(run_ref)

    groups = [slice(j * sub_rows, (j + 1) * sub_rows) for j in range(n_sub)]
    from_prompt = i < nbp

    x1s = []
    for rows in groups:
        acc = jnp.dot(a_ref[rows, :], wo_ref[0:half, :], preferred_element_type=F32)
        acc = acc + jnp.dot(hg_ref[rows, :], wo_ref[half:2 * half, :], preferred_element_type=F32)
        x1 = jnp.where(from_prompt, xp_ref[rows, :], xs_ref[rows, :]) + acc
        x1_ref[rows, :] = x1
        x1s.append(x1)

    logits_all = []
    for rows, x1 in zip(groups, x1s):
        h2 = x1 * lax.rsqrt(jnp.mean(x1 * x1, axis=-1, keepdims=True) + EPS) * n2_ref[...]
        for j, words in enumerate(_pack_row(h2)):
            h2_ref[pl.ds(rows.start * ROW_PLANES + j, sub_rows, stride=ROW_PLANES), :] = words
        h_hi = h2.astype(BF16)
        h_lo = (h2 - h_hi.astype(F32)).astype(BF16)
        logits_all.append((jnp.dot(h_hi, wrh_ref[...], preferred_element_type=F32)
                           + jnp.dot(h_lo, wrh_ref[...], preferred_element_type=F32)
                           + jnp.dot(h_hi, wrl_ref[...], preferred_element_type=F32)) + br_ref[...])

    lane = lax.broadcasted_iota(jnp.int32, logits_all[0].shape, 1).astype(F32)
    far = float(LANE)
    is_g = lane < N_GROUPS
    picks = []
    for logits in logits_all:
        gl = jnp.where(is_g, logits, -jnp.inf)
        gmax = jnp.max(gl, axis=-1, keepdims=True)
        gidx = jnp.min(jnp.where(gl == gmax, lane, far), axis=-1, keepdims=True)
        g_w = 1.0 / jnp.sum(jnp.where(is_g, jnp.exp(logits - gmax), 0.0), axis=-1, keepdims=True)
        lo = N_GROUPS + gidx * EXPERTS_PER_GROUP
        in_grp = (lane >= lo) & (lane < lo + EXPERTS_PER_GROUP)
        el = jnp.where(in_grp, logits, -jnp.inf)
        m1 = jnp.max(el, axis=-1, keepdims=True)
        i1 = jnp.min(jnp.where(el == m1, lane, far), axis=-1, keepdims=True)
        el2 = jnp.where(lane == i1, -jnp.inf, el)
        m2 = jnp.max(el2, axis=-1, keepdims=True)
        i2 = jnp.min(jnp.where(el2 == m2, lane, far), axis=-1, keepdims=True)
        p2 = jnp.exp(m2 - m1)
        picks.append((i1, i2, g_w * (1.0 / (1.0 + p2)), g_w * (p2 / (1.0 + p2))))

    run = run_ref[...]
    for rows, (i1, i2, c1, c2) in zip(groups, picks):
        oh1 = lane == i1
        oh2 = lane == i2
        one1 = jnp.where(oh1, 1.0, 0.0)
        one2 = jnp.where(oh2, 1.0, 0.0)
        before1 = jnp.dot(tri_ref[...], one1.astype(BF16), preferred_element_type=F32)
        before2 = jnp.dot(tri_ref[...], one2.astype(BF16), preferred_element_type=F32)
        tot1 = jnp.sum(one1, axis=0, keepdims=True)
        r1 = jnp.sum(jnp.where(oh1, run + before1, 0.0), axis=-1, keepdims=True)
        r2 = jnp.sum(jnp.where(oh2, run + tot1 + before2, 0.0), axis=-1, keepdims=True)
        run = run + tot1 + jnp.sum(one2, axis=0, keepdims=True)
        ri = jnp.where(lane == 0.0, i1 - N_GROUPS, jnp.where(lane == 1.0, i2 - N_GROUPS,
                       jnp.where(lane == 2.0, r1, jnp.where(lane == 3.0, r2, 0.0))))
        ri_ref[rows, :] = ri.astype(jnp.int32)
        rw_ref[rows, :] = jnp.where(lane == 0.0, c1, jnp.where(lane == 1.0, c2, 0.0))
    run_ref[...] = run
    cnt_ref[...] = run.astype(jnp.int32)


def _outproj(xp, xs, attn, hg, wo_bf16, norm2_w, wr_hi, wr_lo, br):
    tp, ts = xp.shape[0], xs.shape[0]
    t = tp + ts
    tm = _largest_divisor(np.gcd(tp, ts), ROUTE_TILE)
    nbp, nbs = tp // tm, ts // tm
    half = attn.shape[1]
    sub_rows = _largest_divisor(tm, ROUTE_SUB)
    strict_lower = jnp.asarray(np.tril(np.ones((sub_rows, sub_rows), np.float32), -1), BF16)

    def const(shape):
        return pl.BlockSpec(shape, lambda i: (0, 0), pipeline_mode=pl.Buffered(1))

    def rows(width):
        return pl.BlockSpec((tm, width), lambda i: (i, 0))

    return pl.pallas_call(
        functools.partial(_outproj_kernel, nbp),
        grid=(nbp + nbs,),
        in_specs=[
            pl.BlockSpec((tm, D_MODEL), lambda i: (jnp.minimum(i, nbp - 1), 0)),
            pl.BlockSpec((tm, D_MODEL), lambda i: (jnp.maximum(i - nbp, 0), 0)),
            rows(half), rows(half),
            const((2 * half, D_MODEL)), const((1, D_MODEL)),
            const((D_MODEL, LANE)), const((D_MODEL, LANE)), const((1, LANE)), const((sub_rows, sub_rows)),
        ],
        out_specs=[rows(D_MODEL), pl.BlockSpec((tm * ROW_PLANES, LANE), lambda i: (i, 0)),
                   rows(LANE), rows(LANE), const((1, LANE))],
        out_shape=[
            jax.ShapeDtypeStruct((t, D_MODEL), F32),
            jax.ShapeDtypeStruct((t * ROW_PLANES, LANE), U32),
            jax.ShapeDtypeStruct((t, LANE), jnp.int32),
            jax.ShapeDtypeStruct((t, LANE), F32),
            jax.ShapeDtypeStruct((1, LANE), jnp.int32),
        ],
        scratch_shapes=[pltpu.VMEM((1, LANE), F32)],
        compiler_params=_cparams("arbitrary"),
        name="outproj",
    )(xp, xs, attn, hg, wo_bf16, norm2_w, wr_hi, wr_lo, br, strict_lower)


def _plan_expert_tiles(ri, counts, tm):
    t = ri.shape[0]
    n_slots = 2 * t
    n_tiles = n_slots // tm + N_EXPERTS
    p = n_tiles * tm
    ef = jnp.concatenate([ri[:, 0], ri[:, 1]])
    rank = jnp.concatenate([ri[:, 2], ri[:, 3]])
    tiles_per = (counts + tm - 1) // tm
    tile_end = jnp.cumsum(tiles_per)
    seg_start = (tile_end - tiles_per) * tm
    experts = jnp.arange(N_EXPERTS, dtype=jnp.int32)
    ppos = jnp.sum(jnp.where(ef[:, None] == experts[None, :], seg_start[None, :], 0), axis=1) + rank
    slot = jnp.arange(n_slots, dtype=jnp.int32)
    real_dest = jnp.full((p,), -1, jnp.int32).at[ppos].set(slot, unique_indices=True)
    is_pad = real_dest < 0
    pos = jnp.arange(p, dtype=jnp.int32)
    dest = jnp.where(is_pad, n_slots + pos % (2 * tm), real_dest)
    src_tok = jnp.where(is_pad, 0, real_dest % t)
    tiles = jnp.arange(n_tiles, dtype=jnp.int32)
    tile_expert = jnp.minimum(jnp.sum((tile_end[None, :] <= tiles[:, None]).astype(jnp.int32), axis=1),
                              N_EXPERTS - 1)
    n_used = tile_end[N_EXPERTS - 1:].astype(jnp.int32)
    return src_tok.reshape(n_tiles, 1, tm), dest.reshape(n_tiles, 1, tm), tile_expert, n_used, n_tiles


def _experts_kernel(te_ref, nu_ref, src_ref, srcn_ref, dstp_ref, dst_ref, h2_hbm, wg_ref, wu_ref, wd_ref, y_hbm,
                    gbuf, obuf, wg_bf, wu_bf, wd_bf, gsem, ssem):
    i = pl.program_id(0)
    n = nu_ref[0]
    pitch = EXPERT_ROW_PITCH
    n_planes = ROW_PLANES
    tm = gbuf.shape[1] // pitch
    slot = i % 2
    other = 1 - slot
    live = i < n

    def gather_row(idx_ref, k, s):
        src = h2_hbm.at[pl.ds(pl.multiple_of(idx_ref[0, 0, k], n_planes), n_planes), :]
        return pltpu.make_async_copy(src, gbuf.at[s, pl.ds(k * pitch, n_planes), :], gsem.at[s])

    def scatter_row(idx_ref, k, s):
        dst = y_hbm.at[pl.ds(pl.multiple_of(idx_ref[0, 0, k], n_planes), n_planes), :]
        return pltpu.make_async_copy(obuf.at[s, pl.ds(k * pitch, n_planes), :], dst, ssem.at[s])

    def wait_gather(s):
        pltpu.make_async_copy(h2_hbm.at[pl.ds(0, tm * n_planes), :], gbuf.at[s, pl.ds(0, tm * n_planes), :],
                              gsem.at[s]).wait()

    def wait_scatter(s):
        pltpu.make_async_copy(obuf.at[s, pl.ds(0, tm * n_planes), :], y_hbm.at[pl.ds(0, tm * n_planes), :],
                              ssem.at[s]).wait()

    @pl.when(i == 0)
    def _():
        def body(k, carry):
            gather_row(src_ref, k, 0).start()
            return carry
        lax.fori_loop(0, tm, body, 0, unroll=8)
        obuf[1] = jnp.zeros(obuf.shape[1:], obuf.dtype)
        spare0 = y_hbm.shape[0] - 2 * tm * n_planes
        for part in range(2):
            fill = pltpu.make_async_copy(
                obuf.at[1, pl.ds(0, tm * n_planes), :],
                y_hbm.at[pl.ds(spare0 + part * tm * n_planes, tm * n_planes), :], ssem.at[0])
            fill.start()
            fill.wait()

    @pl.when((i >= 1) & live)
    def _():
        wait_scatter(slot)

    changed = live & ((i == 0) | (te_ref[i] != te_ref[jnp.maximum(i - 1, 0)]))

    @pl.when(changed)
    def _():
        wg_bf[...] = wg_ref[0].astype(BF16)
        wu_bf[...] = wu_ref[0].astype(BF16)
        wd_bf[...] = wd_ref[0].astype(BF16)

    def tile_body(s):
        wait_gather(s)
        words = [gbuf[s, pl.ds(j, tm, stride=pitch), :] for j in range(n_planes)]
        hb = jnp.concatenate([p.astype(BF16) for p in _unpack_row(words)], axis=1)
        for k in range(tm):
            gather_row(srcn_ref, k, 1 - s).start()
        for k in range(tm):
            scatter_row(dstp_ref, k, 1 - s).start()
        gate = jnp.dot(hb, wg_bf[...], preferred_element_type=F32)
        up = jnp.dot(hb, wu_bf[...], preferred_element_type=F32)
        act = (gate * _sigmoid(gate) * up).astype(BF16)
        y = jnp.dot(act, wd_bf[...], preferred_element_type=F32)
        for j, out_words in enumerate(_pack_row(y)):
            obuf[s, pl.ds(j, tm, stride=pitch), :] = out_words

    @pl.when((slot == 0) & live)
    def _():
        tile_body(0)

    @pl.when((slot == 1) & live)
    def _():
        tile_body(1)

    @pl.when(i == n - 1)
    def _():
        wait_gather(other)
        wait_scatter(other)

        def body(k, carry):
            scatter_row(dst_ref, k, slot).start()
            return carry
        lax.fori_loop(0, tm, body, 0, unroll=8)
        wait_scatter(slot)


def _experts(h2, src_tok, dest, tile_expert, n_used, n_tiles, w_gate, w_up, w_down):
    tm = EXPERT_TILE
    n_planes = ROW_PLANES
    t = h2.shape[0] // n_planes
    src_tok, dest = src_tok * n_planes, dest * n_planes

    def idx_spec(shift):
        return pl.BlockSpec((1, 1, tm), lambda i, te, nu: (jnp.minimum(i + shift, nu[0] - 1), 0, 0),
                            memory_space=pltpu.SMEM)

    def w_spec(shape):
        return pl.BlockSpec((1,) + shape, lambda i, te, nu: (te[jnp.minimum(i, nu[0] - 1)], 0, 0))

    spare = ((2 * t + jnp.arange(tm, dtype=jnp.int32)) * n_planes).reshape(1, 1, tm)
    dest_prev = jnp.concatenate([spare, dest[:-1]], axis=0)

    grid_spec = pltpu.PrefetchScalarGridSpec(
        num_scalar_prefetch=2,
        grid=(n_tiles,),
        in_specs=[
            idx_spec(0), idx_spec(1), idx_spec(0), idx_spec(0),
            pl.BlockSpec(memory_space=pl.ANY),
            w_spec((D_MODEL, D_EXPERT)), w_spec((D_MODEL, D_EXPERT)), w_spec((D_EXPERT, D_MODEL)),
        ],
        out_specs=pl.BlockSpec(memory_space=pl.ANY),
        scratch_shapes=[
            pltpu.VMEM((2, tm * EXPERT_ROW_PITCH, LANE), U32),
            pltpu.VMEM((2, tm * EXPERT_ROW_PITCH, LANE), U32),
            pltpu.VMEM((D_MODEL, D_EXPERT), BF16),
            pltpu.VMEM((D_MODEL, D_EXPERT), BF16),
            pltpu.VMEM((D_EXPERT, D_MODEL), BF16),
            pltpu.SemaphoreType.DMA((2,)),
            pltpu.SemaphoreType.DMA((2,)),
        ],
    )
    return pl.pallas_call(
        _experts_kernel,
        grid_spec=grid_spec,
        out_shape=jax.ShapeDtypeStruct(((2 * t + 2 * tm) * n_planes, LANE), U32),
        compiler_params=_cparams("arbitrary"),
        name="experts",
    )(tile_expert, n_used, src_tok, src_tok, dest_prev, dest, h2, w_gate, w_up, w_down)


def _final_kernel(x1_ref, ya_ref, yb_ref, rw_ref, nw_ref, o_ref):
    rw = rw_ref[...]
    c1, c2 = rw[:, 0:1], rw[:, 1:2]
    tm = x1_ref.shape[0]
    ya = _unpack_row([ya_ref[pl.ds(j, tm, stride=ROW_PLANES), :] for j in range(ROW_PLANES)])
    yb = _unpack_row([yb_ref[pl.ds(j, tm, stride=ROW_PLANES), :] for j in range(ROW_PLANES)])
    planes = [x1_ref[:, j * LANE:(j + 1) * LANE] + c1 * ya[j] + c2 * yb[j] for j in range(len(ya))]
    sq = planes[0] * planes[0]
    for xj in planes[1:]:
        sq = sq + xj * xj
    r = lax.rsqrt(jnp.sum(sq, axis=-1, keepdims=True) * (1.0 / o_ref.shape[1]) + EPS)
    for j, xj in enumerate(planes):
        cols = slice(j * LANE, (j + 1) * LANE)
        o_ref[:, cols] = xj * r * nw_ref[:, cols]


def _final(x1, y, rw, norm_w, row0, n_rows):
    t = x1.shape[0]
    tm = _largest_divisor(np.gcd(np.gcd(row0, n_rows), t), 512)
    b0 = row0 // tm
    bt = t // tm

    return pl.pallas_call(
        _final_kernel,
        grid=(n_rows // tm,),
        in_specs=[
            pl.BlockSpec((tm, D_MODEL), lambda i: (b0 + i, 0)),
            pl.BlockSpec((tm * ROW_PLANES, LANE), lambda i: (b0 + i, 0)),
            pl.BlockSpec((tm * ROW_PLANES, LANE), lambda i: (bt + b0 + i, 0)),
            pl.BlockSpec((tm, LANE), lambda i: (b0 + i, 0)),
            pl.BlockSpec((1, D_MODEL), lambda i: (0, 0)),
        ],
        out_specs=pl.BlockSpec((tm, D_MODEL), lambda i: (i, 0)),
        out_shape=jax.ShapeDtypeStruct((n_rows, D_MODEL), F32),
        compiler_params=_cparams("parallel"),
        name="final",
    )(x1, y, y, rw, norm_w)


def kernel(x_prompt, x_sample, norm1_w, w_in, attn_sink, hg_lb, hg_norm_w, w_out, norm2_w, w_router_g,
           b_router_g, w_router_e, b_router_e, w_gate, w_up, w_down, final_norm_w):
    bp, seq, d = x_prompt.shape
    bs, seq_s, _ = x_sample.shape
    assert d == D_MODEL and seq == seq_s and seq % max(ATTN_BLOCK, HG_CHUNK) == 0
    assert w_in.shape[0] == 1 and w_in.shape[2] == N_SLABS * LANE
    n_seq = bp + bs
    tp, ts = bp * seq, bs * seq
    xp = x_prompt.reshape(tp, d)
    xs = x_sample.reshape(ts, d)

    proj = _inproj(xp, xs, norm1_w, w_in[0].astype(BF16))
    attn = _attention(proj, attn_sink, n_seq, seq)
    hg = _hgrn(proj, hg_lb, hg_norm_w, n_seq, seq)

    wr = jnp.concatenate([w_router_g[0], w_router_e[0].reshape(d, N_EXPERTS)], axis=1)
    wr = jnp.pad(wr, ((0, 0), (0, LANE - wr.shape[1])))
    wr_hi = wr.astype(BF16)
    wr_lo = (wr - wr_hi.astype(F32)).astype(BF16)
    br = jnp.concatenate([b_router_g[0], b_router_e[0].reshape(N_EXPERTS)])
    br = jnp.pad(br, (0, LANE - br.shape[0])).reshape(1, LANE)
    x1, h2, ri, rw, cnt = _outproj(xp, xs, attn, hg, w_out[0].astype(BF16), norm2_w, wr_hi, wr_lo, br)

    counts = cnt[0, N_GROUPS:N_GROUPS + N_EXPERTS]
    src_tok, dest, tile_expert, n_used, n_tiles = _plan_expert_tiles(ri, counts, EXPERT_TILE)
    y = _experts(h2, src_tok, dest, tile_expert, n_used, n_tiles, w_gate[0], w_up[0], w_down[0])

    fnw = final_norm_w.reshape(1, d)
    y_prompt = _final(x1, y, rw, fnw, 0, tp).reshape(bp, seq, d)
    y_sample = _final(x1, y, rw, fnw, tp, ts).reshape(bs, seq, d)
    return (y_prompt, y_sample)
```

```python
import functools

import numpy as np
import jax
import jax.numpy as jnp
from jax import lax
from jax.experimental import pallas as pl
from jax.experimental.pallas import tpu as pltpu

F32 = jnp.float32
BF16 = jnp.bfloat16
U32 = jnp.uint32

LANE = 128
D_MODEL = 2048
ROW_PLANES = D_MODEL // (2 * LANE)
ATTN_HEADS = 8
ATTN_KV_HEADS = 2
ATTN_GROUP = ATTN_HEADS // ATTN_KV_HEADS
HEAD_DIM = LANE
WINDOW = 128
ATTN_BLOCK = 128
HG_HEADS = 8
HG_CHUNK = 128
HG_VPU_MIN_HALF = 4
HG_CHAIN_GROUP = 8
HG_HEADS_PER_STEP = 4
N_GROUPS = 4
EXPERTS_PER_GROUP = 8
N_EXPERTS = N_GROUPS * EXPERTS_PER_GROUP
D_EXPERT = 512
EPS = 1e-6

SLAB_QA, SLAB_KA, SLAB_VA, SLAB_QH, SLAB_FF, SLAB_FB, SLAB_IH, SLAB_GH = 0, 8, 10, 12, 20, 28, 36, 44
N_SLABS = 52

EXPERT_TILE = 256
EXPERT_ROW_PITCH = ROW_PLANES
ROUTE_TILE = 512
ROUTE_SUB = 128
VMEM_LIMIT = 56 * 1024 * 1024


def _pack_row(x):
    half = D_MODEL // 2
    return [pltpu.pack_elementwise([x[:, i * LANE:(i + 1) * LANE], x[:, half + i * LANE:half + (i + 1) * LANE]],
                                   packed_dtype=BF16) for i in range(ROW_PLANES)]


def _unpack_row(planes):
    return [pltpu.unpack_elementwise(w, index=part, packed_dtype=BF16, unpacked_dtype=F32)
            for part in (0, 1) for w in planes]


def _largest_divisor(n, cap):
    n = int(n)
    d = min(n, cap)
    while n % d:
        d -= 1
    return d


def _cparams(*sem):
    return pltpu.CompilerParams(dimension_semantics=sem, vmem_limit_bytes=VMEM_LIMIT)


def _inproj_kernel(nbp, xp_ref, xs_ref, nw_ref, w_ref, o_ref, h_ref):
    i = pl.program_id(0)
    j = pl.program_id(1)

    def norm_into_scratch(x_ref):
        x = x_ref[...]
        ms = jnp.mean(x * x, axis=-1, keepdims=True)
        h_ref[...] = (x * lax.rsqrt(ms + EPS) * nw_ref[...]).astype(BF16)

    @pl.when((j == 0) & (i < nbp))
    def _():
        norm_into_scratch(xp_ref)

    @pl.when((j == 0) & (i >= nbp))
    def _():
        norm_into_scratch(xs_ref)

    acc = jnp.dot(h_ref[...], w_ref[...], preferred_element_type=F32)
    for c in range(o_ref.shape[0]):
        o_ref[c] = acc[:, c * LANE:(c + 1) * LANE].astype(BF16)


def _inproj(xp, xs, norm_w, w_bf16):
    tp, ts = xp.shape[0], xs.shape[0]
    tm = _largest_divisor(np.gcd(tp, ts), 1024)
    tn = 512
    nbp, nbs = tp // tm, ts // tm
    n_out = w_bf16.shape[1]
    return pl.pallas_call(
        functools.partial(_inproj_kernel, nbp),
        grid=(nbp + nbs, n_out // tn),
        in_specs=[
            pl.BlockSpec((tm, D_MODEL), lambda i, j: (jnp.minimum(i, nbp - 1), 0)),
            pl.BlockSpec((tm, D_MODEL), lambda i, j: (jnp.maximum(i - nbp, 0), 0)),
            pl.BlockSpec((1, D_MODEL), lambda i, j: (0, 0)),
            pl.BlockSpec((D_MODEL, tn), lambda i, j: (0, j)),
        ],
        out_specs=pl.BlockSpec((tn // LANE, tm, LANE), lambda i, j: (j, i, 0)),
        out_shape=jax.ShapeDtypeStruct((n_out // LANE, tp + ts, LANE), BF16),
        scratch_shapes=[pltpu.VMEM((tm, D_MODEL), BF16)],
        compiler_params=_cparams("parallel", "arbitrary"),
        name="inproj",
    )(xp, xs, norm_w, w_bf16)


LOG2E = 1.4426950408889634


def _attn_bias_table():
    blk = ATTN_BLOCK
    qi = np.arange(blk)[:, None]
    kc = np.arange(3 * blk)[None, :]
    dist = np.abs(kc - blk - qi)
    slopes = 2.0 ** (-8.0 * np.arange(1, ATTN_HEADS + 1) / ATTN_HEADS)
    table = np.empty((4, ATTN_HEADS, blk, 3 * blk), np.float32)
    for first in (0, 1):
        for last in (0, 1):
            ok = (dist <= WINDOW) & ((kc >= blk) | (first == 0)) & ((kc < 2 * blk) | (last == 0))
            for h in range(ATTN_HEADS):
                table[2 * first + last, h] = np.where(ok, -slopes[h] * LOG2E * dist, -np.inf)
    return table


def _attn_kernel(q_ref, kp_ref, kc_ref, kn_ref, vp_ref, vc_ref, vn_ref, sink_ref, tab_ref, o_ref):
    n = pl.program_id(1)
    nb = pl.num_programs(1)
    edge = 2 * (n == 0).astype(jnp.int32) + (n == nb - 1).astype(jnp.int32)
    scale2 = HEAD_DIM ** -0.5 * LOG2E
    for g in range(ATTN_KV_HEADS):
        heads = range(g * ATTN_GROUP, (g + 1) * ATTN_GROUP)
        kcat = jnp.concatenate([kp_ref[g], kc_ref[g], kn_ref[g]], axis=0)
        vcat = jnp.concatenate([vp_ref[g], vc_ref[g], vn_ref[g]], axis=0)
        logits = [lax.dot_general(q_ref[h], kcat, (((1,), (1,)), ((), ())), preferred_element_type=F32)
                  * scale2 + tab_ref[edge, h] for h in heads]
        probs = []
        for h, s in zip(heads, logits):
            sink2 = sink_ref[0, h] * LOG2E
            m = jnp.maximum(jnp.max(s, axis=-1, keepdims=True), sink2)
            p = jnp.exp2(s - m)
            denom = jnp.sum(p, axis=-1, keepdims=True) + jnp.exp2(sink2 - m)
            probs.append((p.astype(BF16), 1.0 / denom))
        for h, (p, inv) in zip(heads, probs):
            o = jnp.dot(p, vcat, preferred_element_type=F32) * inv
            o_ref[:, h * HEAD_DIM:(h + 1) * HEAD_DIM] = o.astype(BF16)


def _attention(proj, sink, n_seq, seq):
    nb = seq // ATTN_BLOCK
    t = n_seq * seq
    table = jnp.asarray(_attn_bias_table())

    def kv_spec(slab_pair, shift):
        def imap(b, n):
            return (slab_pair, b * nb + jnp.clip(n + shift, 0, nb - 1), 0)
        return pl.BlockSpec((ATTN_KV_HEADS, ATTN_BLOCK, LANE), imap)

    return pl.pallas_call(
        _attn_kernel,
        grid=(n_seq, nb),
        in_specs=[
            pl.BlockSpec((ATTN_HEADS, ATTN_BLOCK, LANE), lambda b, n: (SLAB_QA // ATTN_HEADS, b * nb + n, 0)),
            kv_spec(SLAB_KA // ATTN_KV_HEADS, -1), kv_spec(SLAB_KA // ATTN_KV_HEADS, 0),
            kv_spec(SLAB_KA // ATTN_KV_HEADS, 1),
            kv_spec(SLAB_VA // ATTN_KV_HEADS, -1), kv_spec(SLAB_VA // ATTN_KV_HEADS, 0),
            kv_spec(SLAB_VA // ATTN_KV_HEADS, 1),
            pl.BlockSpec(memory_space=pltpu.SMEM),
            pl.BlockSpec(table.shape, lambda b, n: (0, 0, 0, 0)),
        ],
        out_specs=pl.BlockSpec((ATTN_BLOCK, ATTN_HEADS * HEAD_DIM), lambda b, n: (b * nb + n, 0)),
        out_shape=jax.ShapeDtypeStruct((t, ATTN_HEADS * HEAD_DIM), BF16),
        compiler_params=_cparams("parallel", "arbitrary"),
        name="attn",
    )(proj, proj, proj, proj, proj, proj, proj, sink, table)


def _hgrn_halves(c):
    halves = []
    h = c // 2
    while h >= 1:
        halves.append(h)
        h //= 2
    return halves


def _hgrn_constants(c, reverse):
    halves = _hgrn_halves(c)
    m = np.zeros((2, c, c), np.float32)
    level = -np.ones((c, c), np.int32)
    sign = np.zeros((sum(h >= HG_VPU_MIN_HALF for h in halves), c, LANE), np.float32)
    for t in range(c):
        m[0, t, :t + 1] = 1
        level[t, t] = 0
    for li, h in enumerate(halves):
        for t in range(c):
            pos = t % (2 * h)
            mid = t - pos + h
            if pos >= h:
                level[t, t - pos:mid] = li + 1
            if h == 2:
                if pos >= h:
                    m[1, t, mid:t + 1] = 1
                else:
                    m[1, t, t + 1:mid] = 1
            if h >= HG_VPU_MIN_HALF:
                sign[li, t, :] = 1.0 if pos >= h else -1.0
    if reverse:
        m = m[:, ::-1, ::-1]
        level = level[::-1, ::-1]
        sign = sign[:, ::-1, :]
    m = m.reshape(-1, c)
    return (np.ascontiguousarray(np.concatenate([m, m], axis=1)), np.ascontiguousarray(level),
            np.ascontiguousarray(sign))


def _sigmoid(x):
    return 0.5 + 0.5 * jnp.tanh(0.5 * x)


def _hgrn_kernel(c, q_ref, ff_ref, fb_ref, i_ref, g_ref, lb_ref, nw_ref,
                 mf_ref, mb_ref, lvf_ref, lvb_ref, sgf_ref, sgb_ref, o_ref, of_ref, ob_ref, qs_ref, st_ref):
    n_heads, seq = q_ref.shape[0], q_ref.shape[1]
    nc = seq // c
    halves = _hgrn_halves(c)
    row = lax.broadcasted_iota(jnp.int32, (c, LANE), 0)
    sub = 8

    lbr = lb_ref[...]
    lbe = jnp.exp(lbr - jnp.max(lbr, axis=0, keepdims=True))
    lb_all = lbe[0] / jnp.sum(lbe, axis=0)

    st_ref[...] = jnp.zeros_like(st_ref)

    def qs_body(it, carry):
        r0 = pl.multiple_of(it * c, c)
        for hd in range(n_heads):
            qp = q_ref[hd, pl.ds(r0, c), :].astype(F32)
            qs_ref[hd, pl.ds(r0, c), :] = qp * _sigmoid(qp)
        return carry

    lax.fori_loop(0, nc, qs_body, 0)

    def gates(ch):
        hd, r0, z_ref, m_ref, _, _, reverse = ch
        d = 1 if reverse else 0
        lb = lb_all[d:d + 1, hd * LANE:(hd + 1) * LANE]
        z = z_ref[hd, pl.ds(r0, c), :].astype(F32)
        sig = _sigmoid(z)
        f = lb + (1.0 - lb) * sig
        k = (1.0 - lb) * (1.0 - sig)
        g = jnp.log2(f)
        g_hi = g.astype(BF16)
        g_lo = (g - g_hi.astype(F32)).astype(BF16)
        e = jnp.dot(m_ref[...], jnp.concatenate([g_hi, g_lo], axis=0), preferred_element_type=F32)
        return dict(f=f, k=k, e=e, b=e[0:c, :], qs=qs_ref[hd, pl.ds(r0, c), :])

    def level(ch, s, li):
        _, _, _, _, lv_ref, sg_ref, reverse = ch
        qs, k, b = s["qs"], s["k"], s["b"]
        if li < 0:
            x_q, x_k = qs.astype(BF16), k.astype(BF16)
            h = 1
        else:
            h = halves[li]
            if h == 1:
                w = jnp.where((row & 1) == (0 if reverse else 1), s["f"], 1.0)
            elif h == 2:
                w = jnp.exp2(s["e"][c:2 * c, :])
            elif h < sub:
                b3 = b.reshape(c // (2 * h), 2 * h, LANE)
                mid = h if reverse else h - 1
                w = jnp.exp2((b3 - b3[:, mid:mid + 1, :]).reshape(c, LANE) * sg_ref[li])
            else:
                parts = []
                for j in range(c // h):
                    mid = (j // 2) * 2 * h + (h if reverse else h - 1)
                    blk, ref_row = b[j * h:(j + 1) * h], b[mid:mid + 1]
                    parts.append(blk - ref_row if (j % 2 == 1) != reverse else ref_row - blk)
                w = jnp.exp2(jnp.concatenate(parts, axis=0))
            if h >= sub:
                base = jnp.concatenate(
                    [(qs if (j % 2 == 1) != reverse else k)[j * h:(j + 1) * h] for j in range(c // h)], axis=0)
            else:
                base = jnp.where((row & h) == (0 if reverse else h), qs, k)
            x_q = x_k = (base * w).astype(BF16)
        p = lax.dot_general(x_q, x_k, (((1,), (1,)), ((), ())), preferred_element_type=F32)
        a = s.get("a")
        new_a = []
        for j in range(c // sub):
            rows = slice(j * sub, (j + 1) * sub)
            if a is None:
                new_a.append(jnp.where(lv_ref[rows, :] == 0, p[rows], 0.0))
            elif h < sub or ((j * sub // h) % 2 == 1) != reverse:
                new_a.append(jnp.where(lv_ref[rows, :] == li + 1, p[rows], a[j]))
            else:
                new_a.append(a[j])
        s["a"] = new_a

    def apply(ch, s):
        hd, r0, _, _, _, _, reverse = ch
        d = 1 if reverse else 0
        qs, k, b = s["qs"], s["k"], s["b"]
        v = i_ref[hd, pl.ds(r0, c), :]
        last = 0 if reverse else c - 1
        b_last = b[last:last + 1, :]
        st = st_ref[hd, d]
        o = jnp.dot(jnp.concatenate(s["a"], axis=0).astype(BF16), v, preferred_element_type=F32)
        o = o + lax.dot_general((qs * jnp.exp2(b)).astype(BF16), st.astype(BF16), (((1,), (1,)), ((), ())),
                                preferred_element_type=F32)
        k_out = (k * jnp.exp2(b_last - b)).astype(BF16)
        st_ref[hd, d] = st * jnp.exp2(b_last) + lax.dot_general(
            v, k_out, (((0,), (0,)), ((), ())), preferred_element_type=F32)
        return o

    def scan_body(it, carry):
        rf = pl.multiple_of(it * c, c)
        rb = pl.multiple_of((nc - 1 - it) * c, c)
        chains = []
        for hd in range(n_heads):
            chains.append((hd, rf, ff_ref, mf_ref, lvf_ref, sgf_ref, False))
            chains.append((hd, rb, fb_ref, mb_ref, lvb_ref, sgb_ref, True))
        for g0 in range(0, len(chains), HG_CHAIN_GROUP):
            group = chains[g0:g0 + HG_CHAIN_GROUP]
            states = [gates(ch) for ch in group]
            for li in range(-1, len(halves)):
                for ch, s in zip(group, states):
                    level(ch, s, li)
            for ch, s in zip(group, states):
                out_ref = ob_ref if ch[6] else of_ref
                out_ref[ch[0], pl.ds(ch[1], c), :] = apply(ch, s)
        return carry

    lax.fori_loop(0, nc, scan_body, 0)

    def out_body(it, carry):
        r0 = pl.multiple_of(it * c, c)
        for hd in range(n_heads):
            tot = of_ref[hd, pl.ds(r0, c), :] + ob_ref[hd, pl.ds(r0, c), :]
            y = tot * lax.rsqrt(jnp.mean(tot * tot, axis=-1, keepdims=True) + EPS) * nw_ref[...]
            gp = g_ref[hd, pl.ds(r0, c), :].astype(F32)
            o_ref[pl.ds(r0, c), hd * LANE:(hd + 1) * LANE] = (y * (gp * _sigmoid(gp))).astype(BF16)
        return carry

    lax.fori_loop(0, nc, out_body, 0)


def _hgrn(proj, hg_lb, norm_w, n_seq, seq):
    c = HG_CHUNK
    hh = HG_HEADS_PER_STEP
    mf, lvf, sgf = _hgrn_constants(c, False)
    mb, lvb, sgb = _hgrn_constants(c, True)
    t = n_seq * seq

    def slab(first):
        return pl.BlockSpec((hh, seq, LANE), lambda b, h: (first // hh + h, b, 0))

    def const(arr):
        return pl.BlockSpec(arr.shape, lambda b, h: (0,) * arr.ndim)

    return pl.pallas_call(
        functools.partial(_hgrn_kernel, c),
        grid=(n_seq, HG_HEADS // hh),
        in_specs=[
            slab(SLAB_QH), slab(SLAB_FF), slab(SLAB_FB), slab(SLAB_IH), slab(SLAB_GH),
            pl.BlockSpec((hg_lb.shape[0], 2, hh * LANE), lambda b, h: (0, 0, h)),
            pl.BlockSpec((1, LANE), lambda b, h: (0, 0)),
            const(mf), const(mb), const(lvf), const(lvb), const(sgf), const(sgb),
        ],
        out_specs=pl.BlockSpec((seq, hh * LANE), lambda b, h: (b, h)),
        out_shape=jax.ShapeDtypeStruct((t, HG_HEADS * LANE), BF16),
        scratch_shapes=[pltpu.VMEM((hh, seq, LANE), F32)] * 3 + [pltpu.VMEM((hh, 2, LANE, LANE), F32)],
        compiler_params=_cparams("parallel", "arbitrary"),
        name="hgrn",
    )(proj, proj, proj, proj, proj, hg_lb, norm_w,
      jnp.asarray(mf, BF16), jnp.asarray(mb, BF16), jnp.asarray(lvf), jnp.asarray(lvb),
      jnp.asarray(sgf), jnp.asarray(sgb))


def _outproj_kernel(nbp, xp_ref, xs_ref, a_ref, hg_ref, wo_ref, n2_ref, wrh_ref, wrl_ref, br_ref, tri_ref,
                    x1_ref, h2_ref, ri_ref, rw_ref, cnt_ref, run_ref):
    i = pl.program_id(0)
    half = a_ref.shape[1]
    sub_rows = tri_ref.shape[0]
    n_sub = a_ref.shape[0] // sub_rows

    @pl.when(i == 0)
    def _():
        run_ref[...] = jnp.zeros_like(run_ref)

    groups = [slice(j * sub_rows, (j + 1) * sub_rows) for j in range(n_sub)]
    from_prompt = i < nbp

    x1s = []
    for rows in groups:
        acc = jnp.dot(a_ref[rows, :], wo_ref[0:half, :], preferred_element_type=F32)
        acc = acc + jnp.dot(hg_ref[rows, :], wo_ref[half:2 * half, :], preferred_element_type=F32)
        x1 = jnp.where(from_prompt, xp_ref[rows, :], xs_ref[rows, :]) + acc
        x1_ref[rows, :] = x1
        x1s.append(x1)

    logits_all = []
    for rows, x1 in zip(groups, x1s):
        h2 = x1 * lax.rsqrt(jnp.mean(x1 * x1, axis=-1, keepdims=True) + EPS) * n2_ref[...]
        for j, words in enumerate(_pack_row(h2)):
            h2_ref[pl.ds(rows.start * ROW_PLANES + j, sub_rows, stride=ROW_PLANES), :] = words
        h_hi = h2.astype(BF16)
        h_lo = (h2 - h_hi.astype(F32)).astype(BF16)
        logits_all.append((jnp.dot(h_hi, wrh_ref[...], preferred_element_type=F32)
                           + jnp.dot(h_lo, wrh_ref[...], preferred_element_type=F32)
                           + jnp.dot(h_hi, wrl_ref[...], preferred_element_type=F32)) + br_ref[...])

    lane = lax.broadcasted_iota(jnp.int32, logits_all[0].shape, 1).astype(F32)
    far = float(LANE)
    is_g = lane < N_GROUPS
    picks = []
    for logits in logits_all:
        gl = jnp.where(is_g, logits, -jnp.inf)
        gmax = jnp.max(gl, axis=-1, keepdims=True)
        gidx = jnp.min(jnp.where(gl == gmax, lane, far), axis=-1, keepdims=True)
        g_w = 1.0 / jnp.sum(jnp.where(is_g, jnp.exp(logits - gmax), 0.0), axis=-1, keepdims=True)
        lo = N_GROUPS + gidx * EXPERTS_PER_GROUP
        in_grp = (lane >= lo) & (lane < lo + EXPERTS_PER_GROUP)
        el = jnp.where(in_grp, logits, -jnp.inf)
        m1 = jnp.max(el, axis=-1, keepdims=True)
        i1 = jnp.min(jnp.where(el == m1, lane, far), axis=-1, keepdims=True)
        el2 = jnp.where(lane == i1, -jnp.inf, el)
        m2 = jnp.max(el2, axis=-1, keepdims=True)
        i2 = jnp.min(jnp.where(el2 == m2, lane, far), axis=-1, keepdims=True)
        p2 = jnp.exp(m2 - m1)
        picks.append((i1, i2, g_w * (1.0 / (1.0 + p2)), g_w * (p2 / (1.0 + p2))))

    run = run_ref[...]
    for rows, (i1, i2, c1, c2) in zip(groups, picks):
        oh1 = lane == i1
        oh2 = lane == i2
        one1 = jnp.where(oh1, 1.0, 0.0)
        one2 = jnp.where(oh2, 1.0, 0.0)
        before1 = jnp.dot(tri_ref[...], one1.astype(BF16), preferred_element_type=F32)
        before2 = jnp.dot(tri_ref[...], one2.astype(BF16), preferred_element_type=F32)
        tot1 = jnp.sum(one1, axis=0, keepdims=True)
        r1 = jnp.sum(jnp.where(oh1, run + before1, 0.0), axis=-1, keepdims=True)
        r2 = jnp.sum(jnp.where(oh2, run + tot1 + before2, 0.0), axis=-1, keepdims=True)
        run = run + tot1 + jnp.sum(one2, axis=0, keepdims=True)
        ri = jnp.where(lane == 0.0, i1 - N_GROUPS, jnp.where(lane == 1.0, i2 - N_GROUPS,
                       jnp.where(lane == 2.0, r1, jnp.where(lane == 3.0, r2, 0.0))))
        ri_ref[rows, :] = ri.astype(jnp.int32)
        rw_ref[rows, :] = jnp.where(lane == 0.0, c1, jnp.where(lane == 1.0, c2, 0.0))
    run_ref[...] = run
    cnt_ref[...] = run.astype(jnp.int32)


def _outproj(xp, xs, attn, hg, wo_bf16, norm2_w, wr_hi, wr_lo, br):
    tp, ts = xp.shape[0], xs.shape[0]
    t = tp + ts
    tm = _largest_divisor(np.gcd(tp, ts), ROUTE_TILE)
    nbp, nbs = tp // tm, ts // tm
    half = attn.shape[1]
    sub_rows = _largest_divisor(tm, ROUTE_SUB)
    strict_lower = jnp.asarray(np.tril(np.ones((sub_rows, sub_rows), np.float32), -1), BF16)

    def const(shape):
        return pl.BlockSpec(shape, lambda i: (0, 0), pipeline_mode=pl.Buffered(1))

    def rows(width):
        return pl.BlockSpec((tm, width), lambda i: (i, 0))

    return pl.pallas_call(
        functools.partial(_outproj_kernel, nbp),
        grid=(nbp + nbs,),
        in_specs=[
            pl.BlockSpec((tm, D_MODEL), lambda i: (jnp.minimum(i, nbp - 1), 0)),
            pl.BlockSpec((tm, D_MODEL), lambda i: (jnp.maximum(i - nbp, 0), 0)),
            rows(half), rows(half),
            const((2 * half, D_MODEL)), const((1, D_MODEL)),
            const((D_MODEL, LANE)), const((D_MODEL, LANE)), const((1, LANE)), const((sub_rows, sub_rows)),
        ],
        out_specs=[rows(D_MODEL), pl.BlockSpec((tm * ROW_PLANES, LANE), lambda i: (i, 0)),
                   rows(LANE), rows(LANE), const((1, LANE))],
        out_shape=[
            jax.ShapeDtypeStruct((t, D_MODEL), F32),
            jax.ShapeDtypeStruct((t * ROW_PLANES, LANE), U32),
            jax.ShapeDtypeStruct((t, LANE), jnp.int32),
            jax.ShapeDtypeStruct((t, LANE), F32),
            jax.ShapeDtypeStruct((1, LANE), jnp.int32),
        ],
        scratch_shapes=[pltpu.VMEM((1, LANE), F32)],
        compiler_params=_cparams("arbitrary"),
        name="outproj",
    )(xp, xs, attn, hg, wo_bf16, norm2_w, wr_hi, wr_lo, br, strict_lower)


def _plan_expert_tiles(ri, counts, tm):
    t = ri.shape[0]
    n_slots = 2 * t
    n_tiles = n_slots // tm + N_EXPERTS
    p = n_tiles * tm
    ef = jnp.concatenate([ri[:, 0], ri[:, 1]])
    rank = jnp.concatenate([ri[:, 2], ri[:, 3]])
    tiles_per = (counts + tm - 1) // tm
    tile_end = jnp.cumsum(tiles_per)
    seg_start = (tile_end - tiles_per) * tm
    experts = jnp.arange(N_EXPERTS, dtype=jnp.int32)
    ppos = jnp.sum(jnp.where(ef[:, None] == experts[None, :], seg_start[None, :], 0), axis=1) + rank
    slot = jnp.arange(n_slots, dtype=jnp.int32)
    real_dest = jnp.full((p,), -1, jnp.int32).at[ppos].set(slot, unique_indices=True)
    is_pad = real_dest < 0
    pos = jnp.arange(p, dtype=jnp.int32)
    dest = jnp.where(is_pad, n_slots + pos % (2 * tm), real_dest)
    src_tok = jnp.where(is_pad, 0, real_dest % t)
    tiles = jnp.arange(n_tiles, dtype=jnp.int32)
    tile_expert = jnp.minimum(jnp.sum((tile_end[None, :] <= tiles[:, None]).astype(jnp.int32), axis=1),
                              N_EXPERTS - 1)
    n_used = tile_end[N_EXPERTS - 1:].astype(jnp.int32)
    return src_tok.reshape(n_tiles, 1, tm), dest.reshape(n_tiles, 1, tm), tile_expert, n_used, n_tiles


def _experts_kernel(te_ref, nu_ref, src_ref, srcn_ref, dstp_ref, dst_ref, h2_hbm, wg_ref, wu_ref, wd_ref, y_hbm,
                    gbuf, obuf, wg_bf, wu_bf, wd_bf, gsem, ssem):
    i = pl.program_id(0)
    n = nu_ref[0]
    pitch = EXPERT_ROW_PITCH
    n_planes = ROW_PLANES
    tm = gbuf.shape[1] // pitch
    slot = i % 2
    other = 1 - slot
    live = i < n

    def gather_row(idx_ref, k, s):
        src = h2_hbm.at[pl.ds(pl.multiple_of(idx_ref[0, 0, k], n_planes), n_planes), :]
        return pltpu.make_async_copy(src, gbuf.at[s, pl.ds(k * pitch, n_planes), :], gsem.at[s])

    def scatter_row(idx_ref, k, s):
        dst = y_hbm.at[pl.ds(pl.multiple_of(idx_ref[0, 0, k], n_planes), n_planes), :]
        return pltpu.make_async_copy(obuf.at[s, pl.ds(k * pitch, n_planes), :], dst, ssem.at[s])

    def wait_gather(s):
        pltpu.make_async_copy(h2_hbm.at[pl.ds(0, tm * n_planes), :], gbuf.at[s, pl.ds(0, tm * n_planes), :],
                              gsem.at[s]).wait()

    def wait_scatter(s):
        pltpu.make_async_copy(obuf.at[s, pl.ds(0, tm * n_planes), :], y_hbm.at[pl.ds(0, tm * n_planes), :],
                              ssem.at[s]).wait()

    @pl.when(i == 0)
    def _():
        def body(k, carry):
            gather_row(src_ref, k, 0).start()
            return carry
        lax.fori_loop(0, tm, body, 0, unroll=8)
        obuf[1] = jnp.zeros(obuf.shape[1:], obuf.dtype)
        spare0 = y_hbm.shape[0] - 2 * tm * n_planes
        for part in range(2):
            fill = pltpu.make_async_copy(
                obuf.at[1, pl.ds(0, tm * n_planes), :],
                y_hbm.at[pl.ds(spare0 + part * tm * n_planes, tm * n_planes), :], ssem.at[0])
            fill.start()
            fill.wait()

    @pl.when((i >= 1) & live)
    def _():
        wait_scatter(slot)

    changed = live & ((i == 0) | (te_ref[i] != te_ref[jnp.maximum(i - 1, 0)]))

    @pl.when(changed)
    def _():
        wg_bf[...] = wg_ref[0].astype(BF16)
        wu_bf[...] = wu_ref[0].astype(BF16)
        wd_bf[...] = wd_ref[0].astype(BF16)

    def tile_body(s):
        wait_gather(s)
        words = [gbuf[s, pl.ds(j, tm, stride=pitch), :] for j in range(n_planes)]
        hb = jnp.concatenate([p.astype(BF16) for p in _unpack_row(words)], axis=1)
        for k in range(tm):
            gather_row(srcn_ref, k, 1 - s).start(priority=k % 2)
        for k in range(tm):
            scatter_row(dstp_ref, k, 1 - s).start(priority=(k + 1) % 2)
        gate = jnp.dot(hb, wg_bf[...], preferred_element_type=F32)
        up = jnp.dot(hb, wu_bf[...], preferred_element_type=F32)
        act = (gate * _sigmoid(gate) * up).astype(BF16)
        y = jnp.dot(act, wd_bf[...], preferred_element_type=F32)
        for j, out_words in enumerate(_pack_row(y)):
            obuf[s, pl.ds(j, tm, stride=pitch), :] = out_words

    @pl.when((slot == 0) & live)
    def _():
        tile_body(0)

    @pl.when((slot == 1) & live)
    def _():
        tile_body(1)

    @pl.when(i == n - 1)
    def _():
        wait_gather(other)
        wait_scatter(other)

        def body(k, carry):
            scatter_row(dst_ref, k, slot).start()
            return carry
        lax.fori_loop(0, tm, body, 0, unroll=8)
        wait_scatter(slot)


def _experts(h2, src_tok, dest, tile_expert, n_used, n_tiles, w_gate, w_up, w_down):
    tm = EXPERT_TILE
    n_planes = ROW_PLANES
    t = h2.shape[0] // n_planes
    src_tok, dest = src_tok * n_planes, dest * n_planes

    def idx_spec(shift):
        return pl.BlockSpec((1, 1, tm), lambda i, te, nu: (jnp.minimum(i + shift, nu[0] - 1), 0, 0),
                            memory_space=pltpu.SMEM)

    def w_spec(shape):
        return pl.BlockSpec((1,) + shape, lambda i, te, nu: (te[jnp.minimum(i, nu[0] - 1)], 0, 0))

    spare = ((2 * t + jnp.arange(tm, dtype=jnp.int32)) * n_planes).reshape(1, 1, tm)
    dest_prev = jnp.concatenate([spare, dest[:-1]], axis=0)

    grid_spec = pltpu.PrefetchScalarGridSpec(
        num_scalar_prefetch=2,
        grid=(n_tiles,),
        in_specs=[
            idx_spec(0), idx_spec(1), idx_spec(0), idx_spec(0),
            pl.BlockSpec(memory_space=pl.ANY),
            w_spec((D_MODEL, D_EXPERT)), w_spec((D_MODEL, D_EXPERT)), w_spec((D_EXPERT, D_MODEL)),
        ],
        out_specs=pl.BlockSpec(memory_space=pl.ANY),
        scratch_shapes=[
            pltpu.VMEM((2, tm * EXPERT_ROW_PITCH, LANE), U32),
            pltpu.VMEM((2, tm * EXPERT_ROW_PITCH, LANE), U32),
            pltpu.VMEM((D_MODEL, D_EXPERT), BF16),
            pltpu.VMEM((D_MODEL, D_EXPERT), BF16),
            pltpu.VMEM((D_EXPERT, D_MODEL), BF16),
            pltpu.SemaphoreType.DMA((2,)),
            pltpu.SemaphoreType.DMA((2,)),
        ],
    )
    return pl.pallas_call(
        _experts_kernel,
        grid_spec=grid_spec,
        out_shape=jax.ShapeDtypeStruct(((2 * t + 2 * tm) * n_planes, LANE), U32),
        compiler_params=_cparams("arbitrary"),
        name="experts",
    )(tile_expert, n_used, src_tok, src_tok, dest_prev, dest, h2, w_gate, w_up, w_down)


def _final_kernel(x1_ref, ya_ref, yb_ref, rw_ref, nw_ref, o_ref):
    rw = rw_ref[...]
    c1, c2 = rw[:, 0:1], rw[:, 1:2]
    tm = x1_ref.shape[0]
    ya = _unpack_row([ya_ref[pl.ds(j, tm, stride=ROW_PLANES), :] for j in range(ROW_PLANES)])
    yb = _unpack_row([yb_ref[pl.ds(j, tm, stride=ROW_PLANES), :] for j in range(ROW_PLANES)])
    planes = [x1_ref[:, j * LANE:(j + 1) * LANE] + c1 * ya[j] + c2 * yb[j] for j in range(len(ya))]
    sq = planes[0] * planes[0]
    for xj in planes[1:]:
        sq = sq + xj * xj
    r = lax.rsqrt(jnp.sum(sq, axis=-1, keepdims=True) * (1.0 / o_ref.shape[1]) + EPS)
    for j, xj in enumerate(planes):
        cols = slice(j * LANE, (j + 1) * LANE)
        o_ref[:, cols] = xj * r * nw_ref[:, cols]


def _final(x1, y, rw, norm_w, row0, n_rows):
    t = x1.shape[0]
    tm = _largest_divisor(np.gcd(np.gcd(row0, n_rows), t), 512)
    b0 = row0 // tm
    bt = t // tm

    return pl.pallas_call(
        _final_kernel,
        grid=(n_rows // tm,),
        in_specs=[
            pl.BlockSpec((tm, D_MODEL), lambda i: (b0 + i, 0)),
            pl.BlockSpec((tm * ROW_PLANES, LANE), lambda i: (b0 + i, 0)),
            pl.BlockSpec((tm * ROW_PLANES, LANE), lambda i: (bt + b0 + i, 0)),
            pl.BlockSpec((tm, LANE), lambda i: (b0 + i, 0)),
            pl.BlockSpec((1, D_MODEL), lambda i: (0, 0)),
        ],
        out_specs=pl.BlockSpec((tm, D_MODEL), lambda i: (i, 0)),
        out_shape=jax.ShapeDtypeStruct((n_rows, D_MODEL), F32),
        compiler_params=_cparams("parallel"),
        name="final",
    )(x1, y, y, rw, norm_w)


def kernel(x_prompt, x_sample, norm1_w, w_in, attn_sink, hg_lb, hg_norm_w, w_out, norm2_w, w_router_g,
           b_router_g, w_router_e, b_router_e, w_gate, w_up, w_down, final_norm_w):
    bp, seq, d = x_prompt.shape
    bs, seq_s, _ = x_sample.shape
    assert d == D_MODEL and seq == seq_s and seq % max(ATTN_BLOCK, HG_CHUNK) == 0
    assert w_in.shape[0] == 1 and w_in.shape[2] == N_SLABS * LANE
    n_seq = bp + bs
    tp, ts = bp * seq, bs * seq
    xp = x_prompt.reshape(tp, d)
    xs = x_sample.reshape(ts, d)

    proj = _inproj(xp, xs, norm1_w, w_in[0].astype(BF16))
    attn = _attention(proj, attn_sink, n_seq, seq)
    hg = _hgrn(proj, hg_lb, hg_norm_w, n_seq, seq)

    wr = jnp.concatenate([w_router_g[0], w_router_e[0].reshape(d, N_EXPERTS)], axis=1)
    wr = jnp.pad(wr, ((0, 0), (0, LANE - wr.shape[1])))
    wr_hi = wr.astype(BF16)
    wr_lo = (wr - wr_hi.astype(F32)).astype(BF16)
    br = jnp.concatenate([b_router_g[0], b_router_e[0].reshape(N_EXPERTS)])
    br = jnp.pad(br, (0, LANE - br.shape[0])).reshape(1, LANE)
    x1, h2, ri, rw, cnt = _outproj(xp, xs, attn, hg, w_out[0].astype(BF16), norm2_w, wr_hi, wr_lo, br)

    counts = cnt[0, N_GROUPS:N_GROUPS + N_EXPERTS]
    src_tok, dest, tile_expert, n_used, n_tiles = _plan_expert_tiles(ri, counts, EXPERT_TILE)
    y = _experts(h2, src_tok, dest, tile_expert, n_used, n_tiles, w_gate[0], w_up[0], w_down[0])

    fnw = final_norm_w.reshape(1, d)
    y_prompt = _final(x1, y, rw, fnw, 0, tp).reshape(bp, seq, d)
    y_sample = _final(x1, y, rw, fnw, tp, ts).reshape(bs, seq, d)
    return (y_prompt, y_sample)
```

```python
import functools

import numpy as np
import jax
import jax.numpy as jnp
from jax import lax
from jax.experimental import pallas as pl
from jax.experimental.pallas import tpu as pltpu

F32 = jnp.float32
BF16 = jnp.bfloat16
U32 = jnp.uint32

LANE = 128
D_MODEL = 2048
ROW_PLANES = D_MODEL // (2 * LANE)
ATTN_HEADS = 8
ATTN_KV_HEADS = 2
ATTN_GROUP = ATTN_HEADS // ATTN_KV_HEADS
HEAD_DIM = LANE
WINDOW = 128
ATTN_BLOCK = 128
ATTN_BLOCKS_PER_STEP = 2
HG_HEADS = 8
HG_CHUNK = 128
HG_VPU_MIN_HALF = 4
HG_CHAIN_GROUP = 8
HG_HEADS_PER_STEP = 4
N_GROUPS = 4
EXPERTS_PER_GROUP = 8
N_EXPERTS = N_GROUPS * EXPERTS_PER_GROUP
D_EXPERT = 512
EPS = 1e-6

SLAB_QA, SLAB_KA, SLAB_VA, SLAB_QH, SLAB_FF, SLAB_FB, SLAB_IH, SLAB_GH = 0, 8, 10, 12, 20, 28, 36, 44
N_SLABS = 52

INPROJ_ROWS = 1024
INPROJ_SLABS = 4
EXPERT_TILE = 256
EXPERT_ROW_PITCH = ROW_PLANES
ROUTE_TILE = 512
ROUTE_SUB = 128
VMEM_LIMIT = 56 * 1024 * 1024


def _pack_row(x):
    half = D_MODEL // 2
    return [pltpu.pack_elementwise([x[:, i * LANE:(i + 1) * LANE], x[:, half + i * LANE:half + (i + 1) * LANE]],
                                   packed_dtype=BF16) for i in range(ROW_PLANES)]


def _unpack_row(planes):
    return [pltpu.unpack_elementwise(w, index=part, packed_dtype=BF16, unpacked_dtype=F32)
            for part in (0, 1) for w in planes]


def _largest_divisor(n, cap):
    n = int(n)
    d = min(n, cap)
    while n % d:
        d -= 1
    return d


def _cparams(*sem):
    return pltpu.CompilerParams(dimension_semantics=sem, vmem_limit_bytes=VMEM_LIMIT)


def _inproj_kernel(nbp, xp_ref, xs_ref, nw_ref, w_ref, o_ref, h_ref):
    i = pl.program_id(0)
    j = pl.program_id(1)

    def norm_into_scratch(x_ref):
        x = x_ref[...]
        ms = jnp.mean(x * x, axis=-1, keepdims=True)
        h_ref[...] = (x * lax.rsqrt(ms + EPS) * nw_ref[...]).astype(BF16)

    @pl.when((j == 0) & (i < nbp))
    def _():
        norm_into_scratch(xp_ref)

    @pl.when((j == 0) & (i >= nbp))
    def _():
        norm_into_scratch(xs_ref)

    acc = jnp.dot(h_ref[...], w_ref[...], preferred_element_type=F32)
    for c in range(o_ref.shape[0]):
        o_ref[c] = acc[:, c * LANE:(c + 1) * LANE].astype(BF16)


def _inproj(xp, xs, norm_w, w_bf16):
    tp, ts = xp.shape[0], xs.shape[0]
    tm = _largest_divisor(np.gcd(tp, ts), INPROJ_ROWS)
    n_out = w_bf16.shape[1]
    tn = LANE * _largest_divisor(n_out // LANE, INPROJ_SLABS)
    nbp, nbs = tp // tm, ts // tm
    return pl.pallas_call(
        functools.partial(_inproj_kernel, nbp),
        grid=(nbp + nbs, n_out // tn),
        in_specs=[
            pl.BlockSpec((tm, D_MODEL), lambda i, j: (jnp.minimum(i, nbp - 1), 0)),
            pl.BlockSpec((tm, D_MODEL), lambda i, j: (jnp.maximum(i - nbp, 0), 0)),
            pl.BlockSpec((1, D_MODEL), lambda i, j: (0, 0)),
            pl.BlockSpec((D_MODEL, tn), lambda i, j: (0, j)),
        ],
        out_specs=pl.BlockSpec((tn // LANE, tm, LANE), lambda i, j: (j, i, 0)),
        out_shape=jax.ShapeDtypeStruct((n_out // LANE, tp + ts, LANE), BF16),
        scratch_shapes=[pltpu.VMEM((tm, D_MODEL), BF16)],
        compiler_params=_cparams("parallel", "arbitrary"),
        name="inproj",
    )(xp, xs, norm_w, w_bf16)


LOG2E = 1.4426950408889634


def _attn_bias_table():
    blk = ATTN_BLOCK
    qi = np.arange(blk)[:, None]
    kc = np.arange(3 * blk)[None, :]
    dist = np.abs(kc - blk - qi)
    slopes = 2.0 ** (-8.0 * np.arange(1, ATTN_HEADS + 1) / ATTN_HEADS)
    table = np.empty((4, ATTN_HEADS, blk, 3 * blk), np.float32)
    for first in (0, 1):
        for last in (0, 1):
            ok = (dist <= WINDOW) & ((kc >= blk) | (first == 0)) & ((kc < 2 * blk) | (last == 0))
            for h in range(ATTN_HEADS):
                table[2 * first + last, h] = np.where(ok, -slopes[h] * LOG2E * dist, -np.inf)
    return table


def _attn_kernel(n_sub, q_ref, *refs):
    k_refs, v_refs = refs[:n_sub + 2], refs[n_sub + 2:2 * n_sub + 4]
    sink_ref, tab_ref, o_ref = refs[2 * n_sub + 4:]
    n = pl.program_id(1)
    last_step = n == pl.num_programs(1) - 1
    blk = ATTN_BLOCK
    scale2 = HEAD_DIM ** -0.5 * LOG2E
    for g in range(ATTN_KV_HEADS):
        heads = range(g * ATTN_GROUP, (g + 1) * ATTN_GROUP)
        for sub in range(n_sub):
            rows = slice(sub * blk, (sub + 1) * blk)
            first = (n == 0).astype(jnp.int32) if sub == 0 else 0
            last = last_step.astype(jnp.int32) if sub == n_sub - 1 else 0
            edge = 2 * first + last
            kcat = jnp.concatenate([r[g] for r in k_refs[sub:sub + 3]], axis=0)
            vcat = jnp.concatenate([r[g] for r in v_refs[sub:sub + 3]], axis=0)
            logits = [lax.dot_general(q_ref[h, rows, :], kcat, (((1,), (1,)), ((), ())),
                                      preferred_element_type=F32) * scale2 + tab_ref[edge, h] for h in heads]
            probs = []
            for h, s in zip(heads, logits):
                sink2 = sink_ref[0, h] * LOG2E
                m = jnp.maximum(jnp.max(s, axis=-1, keepdims=True), sink2)
                p = jnp.exp2(s - m)
                denom = jnp.sum(p, axis=-1, keepdims=True) + jnp.exp2(sink2 - m)
                probs.append((p.astype(BF16), 1.0 / denom))
            for h, (p, inv) in zip(heads, probs):
                o = jnp.dot(p, vcat, preferred_element_type=F32) * inv
                o_ref[rows, h * HEAD_DIM:(h + 1) * HEAD_DIM] = o.astype(BF16)


def _attention(proj, sink, n_seq, seq):
    nb = seq // ATTN_BLOCK
    n_sub = ATTN_BLOCKS_PER_STEP if nb % ATTN_BLOCKS_PER_STEP == 0 else 1
    steps = nb // n_sub
    t = n_seq * seq
    table = jnp.asarray(_attn_bias_table())

    def kv_spec(slab_pair, shift):
        def imap(b, n):
            return (slab_pair, b * nb + jnp.clip(n * n_sub + shift, 0, nb - 1), 0)
        return pl.BlockSpec((ATTN_KV_HEADS, ATTN_BLOCK, LANE), imap)

    shifts = range(-1, n_sub + 1)
    return pl.pallas_call(
        functools.partial(_attn_kernel, n_sub),
        grid=(n_seq, steps),
        in_specs=[
            pl.BlockSpec((ATTN_HEADS, n_sub * ATTN_BLOCK, LANE),
                         lambda b, n: (SLAB_QA // ATTN_HEADS, b * steps + n, 0)),
            *[kv_spec(SLAB_KA // ATTN_KV_HEADS, s) for s in shifts],
            *[kv_spec(SLAB_VA // ATTN_KV_HEADS, s) for s in shifts],
            pl.BlockSpec(memory_space=pltpu.SMEM),
            pl.BlockSpec(table.shape, lambda b, n: (0, 0, 0, 0)),
        ],
        out_specs=pl.BlockSpec((n_sub * ATTN_BLOCK, ATTN_HEADS * HEAD_DIM), lambda b, n: (b * steps + n, 0)),
        out_shape=jax.ShapeDtypeStruct((t, ATTN_HEADS * HEAD_DIM), BF16),
        compiler_params=_cparams("parallel", "arbitrary"),
        name="attn",
    )(proj, *([proj] * (2 * n_sub + 4)), sink, table)


def _hgrn_halves(c):
    halves = []
    h = c // 2
    while h >= 1:
        halves.append(h)
        h //= 2
    return halves


def _hgrn_constants(c, reverse):
    halves = _hgrn_halves(c)
    m = np.zeros((2, c, c), np.float32)
    level = -np.ones((c, c), np.int32)
    sign = np.zeros((sum(h >= HG_VPU_MIN_HALF for h in halves), c, LANE), np.float32)
    for t in range(c):
        m[0, t, :t + 1] = 1
        level[t, t] = 0
    for li, h in enumerate(halves):
        for t in range(c):
            pos = t % (2 * h)
            mid = t - pos + h
            if pos >= h:
                level[t, t - pos:mid] = li + 1
            if h == 2:
                if pos >= h:
                    m[1, t, mid:t + 1] = 1
                else:
                    m[1, t, t + 1:mid] = 1
            if h >= HG_VPU_MIN_HALF:
                sign[li, t, :] = 1.0 if pos >= h else -1.0
    if reverse:
        m = m[:, ::-1, ::-1]
        level = level[::-1, ::-1]
        sign = sign[:, ::-1, :]
    m = m.reshape(-1, c)
    return (np.ascontiguousarray(np.concatenate([m, m], axis=1)), np.ascontiguousarray(level),
            np.ascontiguousarray(sign))


def _sigmoid(x):
    return 0.5 + 0.5 * jnp.tanh(0.5 * x)


def _hgrn_kernel(c, q_ref, ff_ref, fb_ref, i_ref, g_ref, lb_ref, nw_ref,
                 mf_ref, mb_ref, lvf_ref, lvb_ref, sgf_ref, sgb_ref, o_ref, of_ref, ob_ref, qs_ref, st_ref):
    n_heads, seq = q_ref.shape[0], q_ref.shape[1]
    nc = seq // c
    halves = _hgrn_halves(c)
    row = lax.broadcasted_iota(jnp.int32, (c, LANE), 0)
    sub = 8

    lbr = lb_ref[...]
    lbe = jnp.exp(lbr - jnp.max(lbr, axis=0, keepdims=True))
    lb_all = lbe[0] / jnp.sum(lbe, axis=0)

    st_ref[...] = jnp.zeros_like(st_ref)

    def qs_body(it, carry):
        r0 = pl.multiple_of(it * c, c)
        for hd in range(n_heads):
            qp = q_ref[hd, pl.ds(r0, c), :].astype(F32)
            qs_ref[hd, pl.ds(r0, c), :] = qp * _sigmoid(qp)
        return carry

    lax.fori_loop(0, nc, qs_body, 0)

    def gates(ch):
        hd, r0, z_ref, m_ref, _, _, reverse = ch
        d = 1 if reverse else 0
        lb = lb_all[d:d + 1, hd * LANE:(hd + 1) * LANE]
        z = z_ref[hd, pl.ds(r0, c), :].astype(F32)
        sig = _sigmoid(z)
        f = lb + (1.0 - lb) * sig
        k = (1.0 - lb) * (1.0 - sig)
        g = jnp.log2(f)
        g_hi = g.astype(BF16)
        g_lo = (g - g_hi.astype(F32)).astype(BF16)
        e = jnp.dot(m_ref[...], jnp.concatenate([g_hi, g_lo], axis=0), preferred_element_type=F32)
        return dict(f=f, k=k, e=e, b=e[0:c, :], qs=qs_ref[hd, pl.ds(r0, c), :])

    def level(ch, s, li):
        _, _, _, _, lv_ref, sg_ref, reverse = ch
        qs, k, b = s["qs"], s["k"], s["b"]
        if li < 0:
            x_q, x_k = qs.astype(BF16), k.astype(BF16)
            h = 1
        else:
            h = halves[li]
            if h == 1:
                w = jnp.where((row & 1) == (0 if reverse else 1), s["f"], 1.0)
            elif h == 2:
                w = jnp.exp2(s["e"][c:2 * c, :])
            elif h < sub:
                b3 = b.reshape(c // (2 * h), 2 * h, LANE)
                mid = h if reverse else h - 1
                w = jnp.exp2((b3 - b3[:, mid:mid + 1, :]).reshape(c, LANE) * sg_ref[li])
            else:
                parts = []
                for j in range(c // h):
                    mid = (j // 2) * 2 * h + (h if reverse else h - 1)
                    blk, ref_row = b[j * h:(j + 1) * h], b[mid:mid + 1]
                    parts.append(blk - ref_row if (j % 2 == 1) != reverse else ref_row - blk)
                w = jnp.exp2(jnp.concatenate(parts, axis=0))
            if h >= sub:
                base = jnp.concatenate(
                    [(qs if (j % 2 == 1) != reverse else k)[j * h:(j + 1) * h] for j in range(c // h)], axis=0)
            else:
                base = jnp.where((row & h) == (0 if reverse else h), qs, k)
            x_q = x_k = (base * w).astype(BF16)
        p = lax.dot_general(x_q, x_k, (((1,), (1,)), ((), ())), preferred_element_type=F32)
        a = s.get("a")
        new_a = []
        for j in range(c // sub):
            rows = slice(j * sub, (j + 1) * sub)
            if a is None:
                new_a.append(jnp.where(lv_ref[rows, :] == 0, p[rows], 0.0))
            elif h < sub or ((j * sub // h) % 2 == 1) != reverse:
                new_a.append(jnp.where(lv_ref[rows, :] == li + 1, p[rows], a[j]))
            else:
                new_a.append(a[j])
        s["a"] = new_a

    def apply(ch, s):
        hd, r0, _, _, _, _, reverse = ch
        d = 1 if reverse else 0
        qs, k, b = s["qs"], s["k"], s["b"]
        v = i_ref[hd, pl.ds(r0, c), :]
        last = 0 if reverse else c - 1
        b_last = b[last:last + 1, :]
        st = st_ref[hd, d]
        o = jnp.dot(jnp.concatenate(s["a"], axis=0).astype(BF16), v, preferred_element_type=F32)
        o = o + lax.dot_general((qs * jnp.exp2(b)).astype(BF16), st.astype(BF16), (((1,), (1,)), ((), ())),
                                preferred_element_type=F32)
        k_out = (k * jnp.exp2(b_last - b)).astype(BF16)
        st_ref[hd, d] = st * jnp.exp2(b_last) + lax.dot_general(
            v, k_out, (((0,), (0,)), ((), ())), preferred_element_type=F32)
        return o

    def scan_body(it, carry):
        rf = pl.multiple_of(it * c, c)
        rb = pl.multiple_of((nc - 1 - it) * c, c)
        chains = []
        for hd in range(n_heads):
            chains.append((hd, rf, ff_ref, mf_ref, lvf_ref, sgf_ref, False))
            chains.append((hd, rb, fb_ref, mb_ref, lvb_ref, sgb_ref, True))
        for g0 in range(0, len(chains), HG_CHAIN_GROUP):
            group = chains[g0:g0 + HG_CHAIN_GROUP]
            states = [gates(ch) for ch in group]
            for li in range(-1, len(halves)):
                for ch, s in zip(group, states):
                    level(ch, s, li)
            for ch, s in zip(group, states):
                out_ref = ob_ref if ch[6] else of_ref
                out_ref[ch[0], pl.ds(ch[1], c), :] = apply(ch, s)
        return carry

    lax.fori_loop(0, nc, scan_body, 0)

    def out_body(it, carry):
        r0 = pl.multiple_of(it * c, c)
        for hd in range(n_heads):
            tot = of_ref[hd, pl.ds(r0, c), :] + ob_ref[hd, pl.ds(r0, c), :]
            y = tot * lax.rsqrt(jnp.mean(tot * tot, axis=-1, keepdims=True) + EPS) * nw_ref[...]
            gp = g_ref[hd, pl.ds(r0, c), :].astype(F32)
            o_ref[pl.ds(r0, c), hd * LANE:(hd + 1) * LANE] = (y * (gp * _sigmoid(gp))).astype(BF16)
        return carry

    lax.fori_loop(0, nc, out_body, 0)


def _hgrn(proj, hg_lb, norm_w, n_seq, seq):
    c = HG_CHUNK
    hh = HG_HEADS_PER_STEP
    mf, lvf, sgf = _hgrn_constants(c, False)
    mb, lvb, sgb = _hgrn_constants(c, True)
    t = n_seq * seq

    def slab(first):
        return pl.BlockSpec((hh, seq, LANE), lambda b, h: (first // hh + h, b, 0))

    def const(arr):
        return pl.BlockSpec(arr.shape, lambda b, h: (0,) * arr.ndim)

    return pl.pallas_call(
        functools.partial(_hgrn_kernel, c),
        grid=(n_seq, HG_HEADS // hh),
        in_specs=[
            slab(SLAB_QH), slab(SLAB_FF), slab(SLAB_FB), slab(SLAB_IH), slab(SLAB_GH),
            pl.BlockSpec((hg_lb.shape[0], 2, hh * LANE), lambda b, h: (0, 0, h)),
            pl.BlockSpec((1, LANE), lambda b, h: (0, 0)),
            const(mf), const(mb), const(lvf), const(lvb), const(sgf), const(sgb),
        ],
        out_specs=pl.BlockSpec((seq, hh * LANE), lambda b, h: (b, h)),
        out_shape=jax.ShapeDtypeStruct((t, HG_HEADS * LANE), BF16),
        scratch_shapes=[pltpu.VMEM((hh, seq, LANE), F32)] * 3 + [pltpu.VMEM((hh, 2, LANE, LANE), F32)],
        compiler_params=_cparams("parallel", "arbitrary"),
        name="hgrn",
    )(proj, proj, proj, proj, proj, hg_lb, norm_w,
      jnp.asarray(mf, BF16), jnp.asarray(mb, BF16), jnp.asarray(lvf), jnp.asarray(lvb),
      jnp.asarray(sgf), jnp.asarray(sgb))


def _outproj_kernel(nbp, xp_ref, xs_ref, a_ref, hg_ref, wo_ref, n2_ref, wrh_ref, wrl_ref, br_ref, tri_ref,
                    x1_ref, h2_ref, ri_ref, rw_ref, cnt_ref, run_ref):
    i = pl.program_id(0)
    half = a_ref.shape[1]
    sub_rows = tri_ref.shape[0]
    n_sub = a_ref.shape[0] // sub_rows

    @pl.when(i == 0)
    def _():
        run_ref[...] = jnp.zeros_like(run_ref)

    groups = [slice(j * sub_rows, (j + 1) * sub_rows) for j in range(n_sub)]
    from_prompt = i < nbp

    x1s = []
    for rows in groups:
        acc = jnp.dot(a_ref[rows, :], wo_ref[0:half, :], preferred_element_type=F32)
        acc = acc + jnp.dot(hg_ref[rows, :], wo_ref[half:2 * half, :], preferred_element_type=F32)
        x1 = jnp.where(from_prompt, xp_ref[rows, :], xs_ref[rows, :]) + acc
        x1_ref[rows, :] = x1
        x1s.append(x1)

    logits_all = []
    for rows, x1 in zip(groups, x1s):
        h2 = x1 * lax.rsqrt(jnp.mean(x1 * x1, axis=-1, keepdims=True) + EPS) * n2_ref[...]
        for j, words in enumerate(_pack_row(h2)):
            h2_ref[pl.ds(rows.start * ROW_PLANES + j, sub_rows, stride=ROW_PLANES), :] = words
        h_hi = h2.astype(BF16)
        h_lo = (h2 - h_hi.astype(F32)).astype(BF16)
        both = jnp.dot(h_hi, wrl_ref[...], preferred_element_type=F32)
        logits_all.append((both[:, 0:LANE] + both[:, LANE:2 * LANE]
                           + jnp.dot(h_lo, wrh_ref[...], preferred_element_type=F32)) + br_ref[...])

    lane = lax.broadcasted_iota(jnp.int32, logits_all[0].shape, 1).astype(F32)
    far = float(LANE)
    is_g = lane < N_GROUPS
    picks = []
    for logits in logits_all:
        gl = jnp.where(is_g, logits, -jnp.inf)
        gmax = jnp.max(gl, axis=-1, keepdims=True)
        gidx = jnp.min(jnp.where(gl == gmax, lane, far), axis=-1, keepdims=True)
        g_w = 1.0 / jnp.sum(jnp.where(is_g, jnp.exp(logits - gmax), 0.0), axis=-1, keepdims=True)
        lo = N_GROUPS + gidx * EXPERTS_PER_GROUP
        in_grp = (lane >= lo) & (lane < lo + EXPERTS_PER_GROUP)
        el = jnp.where(in_grp, logits, -jnp.inf)
        m1 = jnp.max(el, axis=-1, keepdims=True)
        i1 = jnp.min(jnp.where(el == m1, lane, far), axis=-1, keepdims=True)
        el2 = jnp.where(lane == i1, -jnp.inf, el)
        m2 = jnp.max(el2, axis=-1, keepdims=True)
        i2 = jnp.min(jnp.where(el2 == m2, lane, far), axis=-1, keepdims=True)
        p2 = jnp.exp(m2 - m1)
        picks.append((i1, i2, g_w * (1.0 / (1.0 + p2)), g_w * (p2 / (1.0 + p2))))

    run = run_ref[...]
    for rows, (i1, i2, c1, c2) in zip(groups, picks):
        oh1 = lane == i1
        oh2 = lane == i2
        one1 = jnp.where(oh1, 1.0, 0.0)
        one2 = jnp.where(oh2, 1.0, 0.0)
        before1 = jnp.dot(tri_ref[...], one1.astype(BF16), preferred_element_type=F32)
        before2 = jnp.dot(tri_ref[...], one2.astype(BF16), preferred_element_type=F32)
        tot1 = jnp.sum(one1, axis=0, keepdims=True)
        r1 = jnp.sum(jnp.where(oh1, run + before1, 0.0), axis=-1, keepdims=True)
        r2 = jnp.sum(jnp.where(oh2, run + tot1 + before2, 0.0), axis=-1, keepdims=True)
        run = run + tot1 + jnp.sum(one2, axis=0, keepdims=True)
        ri = jnp.where(lane == 0.0, i1 - N_GROUPS, jnp.where(lane == 1.0, i2 - N_GROUPS,
                       jnp.where(lane == 2.0, r1, jnp.where(lane == 3.0, r2, 0.0))))
        ri_ref[rows, :] = ri.astype(jnp.int32)
        rw_ref[rows, :] = jnp.where(lane == 0.0, c1, jnp.where(lane == 1.0, c2, 0.0))
    run_ref[...] = run
    cnt_ref[...] = run.astype(jnp.int32)


def _outproj(xp, xs, attn, hg, wo_bf16, norm2_w, wr_hi, wr_lo, br):
    tp, ts = xp.shape[0], xs.shape[0]
    t = tp + ts
    tm = _largest_divisor(np.gcd(tp, ts), ROUTE_TILE)
    nbp, nbs = tp // tm, ts // tm
    half = attn.shape[1]
    sub_rows = _largest_divisor(tm, ROUTE_SUB)
    strict_lower = jnp.asarray(np.tril(np.ones((sub_rows, sub_rows), np.float32), -1), BF16)

    def const(shape):
        return pl.BlockSpec(shape, lambda i: (0, 0), pipeline_mode=pl.Buffered(1))

    def rows(width):
        return pl.BlockSpec((tm, width), lambda i: (i, 0))

    return pl.pallas_call(
        functools.partial(_outproj_kernel, nbp),
        grid=(nbp + nbs,),
        in_specs=[
            pl.BlockSpec((tm, D_MODEL), lambda i: (jnp.minimum(i, nbp - 1), 0)),
            pl.BlockSpec((tm, D_MODEL), lambda i: (jnp.maximum(i - nbp, 0), 0)),
            rows(half), rows(half),
            const((2 * half, D_MODEL)), const((1, D_MODEL)),
            const((D_MODEL, LANE)), const((D_MODEL, 2 * LANE)), const((1, LANE)), const((sub_rows, sub_rows)),
        ],
        out_specs=[rows(D_MODEL), pl.BlockSpec((tm * ROW_PLANES, LANE), lambda i: (i, 0)),
                   rows(LANE), rows(LANE), const((1, LANE))],
        out_shape=[
            jax.ShapeDtypeStruct((t, D_MODEL), F32),
            jax.ShapeDtypeStruct((t * ROW_PLANES, LANE), U32),
            jax.ShapeDtypeStruct((t, LANE), jnp.int32),
            jax.ShapeDtypeStruct((t, LANE), F32),
            jax.ShapeDtypeStruct((1, LANE), jnp.int32),
        ],
        scratch_shapes=[pltpu.VMEM((1, LANE), F32)],
        compiler_params=_cparams("arbitrary"),
        name="outproj",
    )(xp, xs, attn, hg, wo_bf16, norm2_w, wr_hi, wr_lo, br, strict_lower)


def _plan_expert_tiles(ri, counts, tm):
    t = ri.shape[0]
    n_slots = 2 * t
    n_tiles = n_slots // tm + N_EXPERTS
    p = n_tiles * tm
    ef = jnp.concatenate([ri[:, 0], ri[:, 1]])
    rank = jnp.concatenate([ri[:, 2], ri[:, 3]])
    tiles_per = (counts + tm - 1) // tm
    tile_end = jnp.cumsum(tiles_per)
    seg_start = (tile_end - tiles_per) * tm
    experts = jnp.arange(N_EXPERTS, dtype=jnp.int32)
    ppos = jnp.sum(jnp.where(ef[:, None] == experts[None, :], seg_start[None, :], 0), axis=1) + rank
    slot = jnp.arange(n_slots, dtype=jnp.int32)
    real_dest = jnp.full((p,), -1, jnp.int32).at[ppos].set(slot, unique_indices=True)
    is_pad = real_dest < 0
    pos = jnp.arange(p, dtype=jnp.int32)
    dest = jnp.where(is_pad, n_slots + pos % (2 * tm), real_dest)
    src_tok = jnp.where(is_pad, 0, real_dest % t)
    tiles = jnp.arange(n_tiles, dtype=jnp.int32)
    tile_expert = jnp.minimum(jnp.sum((tile_end[None, :] <= tiles[:, None]).astype(jnp.int32), axis=1),
                              N_EXPERTS - 1)
    n_used = tile_end[N_EXPERTS - 1:].astype(jnp.int32)
    return src_tok.reshape(n_tiles, 1, tm), dest.reshape(n_tiles, 1, tm), tile_expert, n_used, n_tiles


def _experts_kernel(te_ref, nu_ref, src_ref, srcn_ref, dstp_ref, dst_ref, h2_hbm, wg_ref, wu_ref, wd_ref, y_hbm,
                    gbuf, obuf, wg_bf, wu_bf, wd_bf, gsem, ssem):
    i = pl.program_id(0)
    n = nu_ref[0]
    pitch = EXPERT_ROW_PITCH
    n_planes = ROW_PLANES
    tm = gbuf.shape[1] // pitch
    slot = i % 2
    other = 1 - slot
    live = i < n

    def gather_row(idx_ref, k, s):
        src = h2_hbm.at[pl.ds(pl.multiple_of(idx_ref[0, 0, k], n_planes), n_planes), :]
        return pltpu.make_async_copy(src, gbuf.at[s, pl.ds(k * pitch, n_planes), :], gsem.at[s])

    def scatter_row(idx_ref, k, s):
        dst = y_hbm.at[pl.ds(pl.multiple_of(idx_ref[0, 0, k], n_planes), n_planes), :]
        return pltpu.make_async_copy(obuf.at[s, pl.ds(k * pitch, n_planes), :], dst, ssem.at[s])

    def wait_gather(s):
        pltpu.make_async_copy(h2_hbm.at[pl.ds(0, tm * n_planes), :], gbuf.at[s, pl.ds(0, tm * n_planes), :],
                              gsem.at[s]).wait()

    def wait_scatter(s):
        pltpu.make_async_copy(obuf.at[s, pl.ds(0, tm * n_planes), :], y_hbm.at[pl.ds(0, tm * n_planes), :],
                              ssem.at[s]).wait()

    @pl.when(i == 0)
    def _():
        def body(k, carry):
            gather_row(src_ref, k, 0).start()
            return carry
        lax.fori_loop(0, tm, body, 0, unroll=8)
        obuf[1] = jnp.zeros(obuf.shape[1:], obuf.dtype)
        spare0 = y_hbm.shape[0] - 2 * tm * n_planes
        for part in range(2):
            fill = pltpu.make_async_copy(
                obuf.at[1, pl.ds(0, tm * n_planes), :],
                y_hbm.at[pl.ds(spare0 + part * tm * n_planes, tm * n_planes), :], ssem.at[0])
            fill.start()
            fill.wait()

    @pl.when((i >= 1) & live)
    def _():
        wait_scatter(slot)

    changed = live & ((i == 0) | (te_ref[i] != te_ref[jnp.maximum(i - 1, 0)]))

    @pl.when(changed)
    def _():
        wg_bf[...] = wg_ref[0].astype(BF16)
        wu_bf[...] = wu_ref[0].astype(BF16)
        wd_bf[...] = wd_ref[0].astype(BF16)

    def tile_body(s):
        wait_gather(s)
        words = [gbuf[s, pl.ds(j, tm, stride=pitch), :] for j in range(n_planes)]
        hb = jnp.concatenate([p.astype(BF16) for p in _unpack_row(words)], axis=1)
        for k in range(tm):
            gather_row(srcn_ref, k, 1 - s).start(priority=k % 2)
        for k in range(tm):
            scatter_row(dstp_ref, k, 1 - s).start(priority=(k + 1) % 2)
        gate = jnp.dot(hb, wg_bf[...], preferred_element_type=F32)
        up = jnp.dot(hb, wu_bf[...], preferred_element_type=F32)
        act = (gate * _sigmoid(gate) * up).astype(BF16)
        y = jnp.dot(act, wd_bf[...], preferred_element_type=F32)
        for j, out_words in enumerate(_pack_row(y)):
            obuf[s, pl.ds(j, tm, stride=pitch), :] = out_words

    @pl.when((slot == 0) & live)
    def _():
        tile_body(0)

    @pl.when((slot == 1) & live)
    def _():
        tile_body(1)

    @pl.when(i == n - 1)
    def _():
        wait_gather(other)
        wait_scatter(other)

        def body(k, carry):
            scatter_row(dst_ref, k, slot).start()
            return carry
        lax.fori_loop(0, tm, body, 0, unroll=8)
        wait_scatter(slot)


def _experts(h2, src_tok, dest, tile_expert, n_used, n_tiles, w_gate, w_up, w_down):
    tm = EXPERT_TILE
    n_planes = ROW_PLANES
    t = h2.shape[0] // n_planes
    src_tok, dest = src_tok * n_planes, dest * n_planes

    def idx_spec(shift):
        return pl.BlockSpec((1, 1, tm), lambda i, te, nu: (jnp.minimum(i + shift, nu[0] - 1), 0, 0),
                            memory_space=pltpu.SMEM)

    def w_spec(shape):
        return pl.BlockSpec((1,) + shape, lambda i, te, nu: (te[jnp.minimum(i, nu[0] - 1)], 0, 0))

    spare = ((2 * t + jnp.arange(tm, dtype=jnp.int32)) * n_planes).reshape(1, 1, tm)
    dest_prev = jnp.concatenate([spare, dest[:-1]], axis=0)

    grid_spec = pltpu.PrefetchScalarGridSpec(
        num_scalar_prefetch=2,
        grid=(n_tiles,),
        in_specs=[
            idx_spec(0), idx_spec(1), idx_spec(0), idx_spec(0),
            pl.BlockSpec(memory_space=pl.ANY),
            w_spec((D_MODEL, D_EXPERT)), w_spec((D_MODEL, D_EXPERT)), w_spec((D_EXPERT, D_MODEL)),
        ],
        out_specs=pl.BlockSpec(memory_space=pl.ANY),
        scratch_shapes=[
            pltpu.VMEM((2, tm * EXPERT_ROW_PITCH, LANE), U32),
            pltpu.VMEM((2, tm * EXPERT_ROW_PITCH, LANE), U32),
            pltpu.VMEM((D_MODEL, D_EXPERT), BF16),
            pltpu.VMEM((D_MODEL, D_EXPERT), BF16),
            pltpu.VMEM((D_EXPERT, D_MODEL), BF16),
            pltpu.SemaphoreType.DMA((2,)),
            pltpu.SemaphoreType.DMA((2,)),
        ],
    )
    return pl.pallas_call(
        _experts_kernel,
        grid_spec=grid_spec,
        out_shape=jax.ShapeDtypeStruct(((2 * t + 2 * tm) * n_planes, LANE), U32),
        compiler_params=_cparams("arbitrary"),
        name="experts",
    )(tile_expert, n_used, src_tok, src_tok, dest_prev, dest, h2, w_gate, w_up, w_down)


def _final_kernel(x1_ref, ya_ref, yb_ref, rw_ref, nw_ref, o_ref):
    rw = rw_ref[...]
    c1, c2 = rw[:, 0:1], rw[:, 1:2]
    tm = x1_ref.shape[0]
    ya = _unpack_row([ya_ref[pl.ds(j, tm, stride=ROW_PLANES), :] for j in range(ROW_PLANES)])
    yb = _unpack_row([yb_ref[pl.ds(j, tm, stride=ROW_PLANES), :] for j in range(ROW_PLANES)])
    planes = [x1_ref[:, j * LANE:(j + 1) * LANE] + c1 * ya[j] + c2 * yb[j] for j in range(len(ya))]
    sq = planes[0] * planes[0]
    for xj in planes[1:]:
        sq = sq + xj * xj
    r = lax.rsqrt(jnp.sum(sq, axis=-1, keepdims=True) * (1.0 / o_ref.shape[1]) + EPS)
    for j, xj in enumerate(planes):
        cols = slice(j * LANE, (j + 1) * LANE)
        o_ref[:, cols] = xj * r * nw_ref[:, cols]


def _final(x1, y, rw, norm_w, row0, n_rows):
    t = x1.shape[0]
    tm = _largest_divisor(np.gcd(np.gcd(row0, n_rows), t), 512)
    b0 = row0 // tm
    bt = t // tm

    return pl.pallas_call(
        _final_kernel,
        grid=(n_rows // tm,),
        in_specs=[
            pl.BlockSpec((tm, D_MODEL), lambda i: (b0 + i, 0)),
            pl.BlockSpec((tm * ROW_PLANES, LANE), lambda i: (b0 + i, 0)),
            pl.BlockSpec((tm * ROW_PLANES, LANE), lambda i: (bt + b0 + i, 0)),
            pl.BlockSpec((tm, LANE), lambda i: (b0 + i, 0)),
            pl.BlockSpec((1, D_MODEL), lambda i: (0, 0)),
        ],
        out_specs=pl.BlockSpec((tm, D_MODEL), lambda i: (i, 0)),
        out_shape=jax.ShapeDtypeStruct((n_rows, D_MODEL), F32),
        compiler_params=_cparams("parallel"),
        name="final",
    )(x1, y, y, rw, norm_w)


def kernel(x_prompt, x_sample, norm1_w, w_in, attn_sink, hg_lb, hg_norm_w, w_out, norm2_w, w_router_g,
           b_router_g, w_router_e, b_router_e, w_gate, w_up, w_down, final_norm_w):
    bp, seq, d = x_prompt.shape
    bs, seq_s, _ = x_sample.shape
    assert d == D_MODEL and seq == seq_s and seq % max(ATTN_BLOCK, HG_CHUNK) == 0
    assert w_in.shape[0] == 1 and w_in.shape[2] == N_SLABS * LANE
    n_seq = bp + bs
    tp, ts = bp * seq, bs * seq
    xp = x_prompt.reshape(tp, d)
    xs = x_sample.reshape(ts, d)

    proj = _inproj(xp, xs, norm1_w, w_in[0].astype(BF16))
    attn = _attention(proj, attn_sink, n_seq, seq)
    hg = _hgrn(proj, hg_lb, hg_norm_w, n_seq, seq)

    wr = jnp.concatenate([w_router_g[0], w_router_e[0].reshape(d, N_EXPERTS)], axis=1)
    wr = jnp.pad(wr, ((0, 0), (0, LANE - wr.shape[1])))
    wr_hi = wr.astype(BF16)
    wr_lo = jnp.concatenate([wr_hi, (wr - wr_hi.astype(F32)).astype(BF16)], axis=1)
    br = jnp.concatenate([b_router_g[0], b_router_e[0].reshape(N_EXPERTS)])
    br = jnp.pad(br, (0, LANE - br.shape[0])).reshape(1, LANE)
    x1, h2, ri, rw, cnt = _outproj(xp, xs, attn, hg, w_out[0].astype(BF16), norm2_w, wr_hi, wr_lo, br)

    counts = cnt[0, N_GROUPS:N_GROUPS + N_EXPERTS]
    src_tok, dest, tile_expert, n_used, n_tiles = _plan_expert_tiles(ri, counts, EXPERT_TILE)
    y = _experts(h2, src_tok, dest, tile_expert, n_used, n_tiles, w_gate[0], w_up[0], w_down[0])

    fnw = final_norm_w.reshape(1, d)
    y_prompt = _final(x1, y, rw, fnw, 0, tp).reshape(bp, seq, d)
    y_sample = _final(x1, y, rw, fnw, tp, ts).reshape(bs, seq, d)
    return (y_prompt, y_sample)
```

```python
import functools

import numpy as np
import jax
import jax.numpy as jnp
from jax import lax
from jax.experimental import pallas as pl
from jax.experimental.pallas import tpu as pltpu

F32 = jnp.float32
BF16 = jnp.bfloat16
U32 = jnp.uint32

LANE = 128
D_MODEL = 2048
ROW_PLANES = D_MODEL // (2 * LANE)
ATTN_HEADS = 8
ATTN_KV_HEADS = 2
ATTN_GROUP = ATTN_HEADS // ATTN_KV_HEADS
HEAD_DIM = LANE
WINDOW = 128
ATTN_BLOCK = 128
ATTN_BLOCKS_PER_STEP = 4
HG_HEADS = 8
HG_CHUNK = 128
HG_VPU_MIN_HALF = 4
HG_CHAIN_GROUP = 8
HG_HEADS_PER_STEP = 4
N_GROUPS = 4
EXPERTS_PER_GROUP = 8
N_EXPERTS = N_GROUPS * EXPERTS_PER_GROUP
D_EXPERT = 512
EPS = 1e-6

SLAB_QA, SLAB_KA, SLAB_VA, SLAB_QH, SLAB_FF, SLAB_FB, SLAB_IH, SLAB_GH = 0, 8, 10, 12, 20, 28, 36, 44
N_SLABS = 52

INPROJ_ROWS = 1024
INPROJ_SLABS = 4
EXPERT_TILE = 256
EXPERT_ROW_PITCH = ROW_PLANES
ROUTE_TILE = 512
ROUTE_SUB = 128
VMEM_LIMIT = 56 * 1024 * 1024


def _pack_row(x):
    half = D_MODEL // 2
    return [pltpu.pack_elementwise([x[:, i * LANE:(i + 1) * LANE], x[:, half + i * LANE:half + (i + 1) * LANE]],
                                   packed_dtype=BF16) for i in range(ROW_PLANES)]


def _unpack_row(planes):
    return [pltpu.unpack_elementwise(w, index=part, packed_dtype=BF16, unpacked_dtype=F32)
            for part in (0, 1) for w in planes]


def _largest_divisor(n, cap):
    n = int(n)
    d = min(n, cap)
    while n % d:
        d -= 1
    return d


def _cparams(*sem):
    return pltpu.CompilerParams(dimension_semantics=sem, vmem_limit_bytes=VMEM_LIMIT)


def _inproj_kernel(nbp, xp_ref, xs_ref, nw_ref, w_ref, o_ref, h_ref):
    i = pl.program_id(0)
    j = pl.program_id(1)

    def norm_into_scratch(x_ref):
        x = x_ref[...]
        ms = jnp.mean(x * x, axis=-1, keepdims=True)
        h_ref[...] = (x * lax.rsqrt(ms + EPS) * nw_ref[...]).astype(BF16)

    @pl.when((j == 0) & (i < nbp))
    def _():
        norm_into_scratch(xp_ref)

    @pl.when((j == 0) & (i >= nbp))
    def _():
        norm_into_scratch(xs_ref)

    acc = jnp.dot(h_ref[...], w_ref[...], preferred_element_type=F32)
    for c in range(o_ref.shape[0]):
        o_ref[c] = acc[:, c * LANE:(c + 1) * LANE].astype(BF16)


def _inproj(xp, xs, norm_w, w_bf16):
    tp, ts = xp.shape[0], xs.shape[0]
    tm = _largest_divisor(np.gcd(tp, ts), INPROJ_ROWS)
    n_out = w_bf16.shape[1]
    tn = LANE * _largest_divisor(n_out // LANE, INPROJ_SLABS)
    nbp, nbs = tp // tm, ts // tm
    return pl.pallas_call(
        functools.partial(_inproj_kernel, nbp),
        grid=(nbp + nbs, n_out // tn),
        in_specs=[
            pl.BlockSpec((tm, D_MODEL), lambda i, j: (jnp.minimum(i, nbp - 1), 0)),
            pl.BlockSpec((tm, D_MODEL), lambda i, j: (jnp.maximum(i - nbp, 0), 0)),
            pl.BlockSpec((1, D_MODEL), lambda i, j: (0, 0)),
            pl.BlockSpec((D_MODEL, tn), lambda i, j: (0, j)),
        ],
        out_specs=pl.BlockSpec((tn // LANE, tm, LANE), lambda i, j: (j, i, 0)),
        out_shape=jax.ShapeDtypeStruct((n_out // LANE, tp + ts, LANE), BF16),
        scratch_shapes=[pltpu.VMEM((tm, D_MODEL), BF16)],
        compiler_params=_cparams("parallel", "arbitrary"),
        name="inproj",
    )(xp, xs, norm_w, w_bf16)


LOG2E = 1.4426950408889634


def _attn_bias_table():
    blk = ATTN_BLOCK
    qi = np.arange(blk)[:, None]
    kc = np.arange(3 * blk)[None, :]
    dist = np.abs(kc - blk - qi)
    slopes = 2.0 ** (-8.0 * np.arange(1, ATTN_HEADS + 1) / ATTN_HEADS)
    table = np.empty((4, ATTN_HEADS, blk, 3 * blk), np.float32)
    for first in (0, 1):
        for last in (0, 1):
            ok = (dist <= WINDOW) & ((kc >= blk) | (first == 0)) & ((kc < 2 * blk) | (last == 0))
            for h in range(ATTN_HEADS):
                table[2 * first + last, h] = np.where(ok, -slopes[h] * LOG2E * dist, -np.inf)
    return table


def _attn_kernel(n_sub, q_ref, *refs):
    k_refs, v_refs = refs[:n_sub + 2], refs[n_sub + 2:2 * n_sub + 4]
    sink_ref, tab_ref, o_ref = refs[2 * n_sub + 4:]
    n = pl.program_id(1)
    last_step = n == pl.num_programs(1) - 1
    blk = ATTN_BLOCK
    scale2 = HEAD_DIM ** -0.5 * LOG2E
    for g in range(ATTN_KV_HEADS):
        heads = range(g * ATTN_GROUP, (g + 1) * ATTN_GROUP)
        for sub in range(n_sub):
            rows = slice(sub * blk, (sub + 1) * blk)
            first = (n == 0).astype(jnp.int32) if sub == 0 else 0
            last = last_step.astype(jnp.int32) if sub == n_sub - 1 else 0
            edge = 2 * first + last
            kcat = jnp.concatenate([r[g] for r in k_refs[sub:sub + 3]], axis=0)
            vcat = jnp.concatenate([r[g] for r in v_refs[sub:sub + 3]], axis=0)
            logits = [lax.dot_general(q_ref[h, rows, :], kcat, (((1,), (1,)), ((), ())),
                                      preferred_element_type=F32) * scale2 + tab_ref[edge, h] for h in heads]
            probs = []
            for h, s in zip(heads, logits):
                sink2 = sink_ref[0, h] * LOG2E
                m = jnp.maximum(jnp.max(s, axis=-1, keepdims=True), sink2)
                p = jnp.exp2(s - m)
                denom = jnp.sum(p, axis=-1, keepdims=True) + jnp.exp2(sink2 - m)
                probs.append((p.astype(BF16), 1.0 / denom))
            for h, (p, inv) in zip(heads, probs):
                o = jnp.dot(p, vcat, preferred_element_type=F32) * inv
                o_ref[rows, h * HEAD_DIM:(h + 1) * HEAD_DIM] = o.astype(BF16)


def _attention(proj, sink, n_seq, seq):
    nb = seq // ATTN_BLOCK
    n_sub = ATTN_BLOCKS_PER_STEP if nb % ATTN_BLOCKS_PER_STEP == 0 else 1
    steps = nb // n_sub
    t = n_seq * seq
    table = jnp.asarray(_attn_bias_table())

    def kv_spec(slab_pair, shift):
        def imap(b, n):
            return (slab_pair, b * nb + jnp.clip(n * n_sub + shift, 0, nb - 1), 0)
        return pl.BlockSpec((ATTN_KV_HEADS, ATTN_BLOCK, LANE), imap)

    shifts = range(-1, n_sub + 1)
    return pl.pallas_call(
        functools.partial(_attn_kernel, n_sub),
        grid=(n_seq, steps),
        in_specs=[
            pl.BlockSpec((ATTN_HEADS, n_sub * ATTN_BLOCK, LANE),
                         lambda b, n: (SLAB_QA // ATTN_HEADS, b * steps + n, 0)),
            *[kv_spec(SLAB_KA // ATTN_KV_HEADS, s) for s in shifts],
            *[kv_spec(SLAB_VA // ATTN_KV_HEADS, s) for s in shifts],
            pl.BlockSpec(memory_space=pltpu.SMEM),
            pl.BlockSpec(table.shape, lambda b, n: (0, 0, 0, 0)),
        ],
        out_specs=pl.BlockSpec((n_sub * ATTN_BLOCK, ATTN_HEADS * HEAD_DIM), lambda b, n: (b * steps + n, 0)),
        out_shape=jax.ShapeDtypeStruct((t, ATTN_HEADS * HEAD_DIM), BF16),
        compiler_params=_cparams("parallel", "arbitrary"),
        name="attn",
    )(proj, *([proj] * (2 * n_sub + 4)), sink, table)


def _hgrn_halves(c):
    halves = []
    h = c // 2
    while h >= 1:
        halves.append(h)
        h //= 2
    return halves


def _hgrn_constants(c, reverse):
    halves = _hgrn_halves(c)
    m = np.zeros((2, c, c), np.float32)
    level = -np.ones((c, c), np.int32)
    sign = np.zeros((sum(h >= HG_VPU_MIN_HALF for h in halves), c, LANE), np.float32)
    for t in range(c):
        m[0, t, :t + 1] = 1
        level[t, t] = 0
    for li, h in enumerate(halves):
        for t in range(c):
            pos = t % (2 * h)
            mid = t - pos + h
            if pos >= h:
                level[t, t - pos:mid] = li + 1
            if h == 2:
                if pos >= h:
                    m[1, t, mid:t + 1] = 1
                else:
                    m[1, t, t + 1:mid] = 1
            if h >= HG_VPU_MIN_HALF:
                sign[li, t, :] = 1.0 if pos >= h else -1.0
    if reverse:
        m = m[:, ::-1, ::-1]
        level = level[::-1, ::-1]
        sign = sign[:, ::-1, :]
    m = m.reshape(-1, c)
    return (np.ascontiguousarray(np.concatenate([m, m], axis=1)), np.ascontiguousarray(level),
            np.ascontiguousarray(sign))


def _sigmoid(x):
    return 0.5 + 0.5 * jnp.tanh(0.5 * x)


def _hgrn_kernel(c, q_ref, ff_ref, fb_ref, i_ref, g_ref, lb_ref, nw_ref,
                 mf_ref, mb_ref, lvf_ref, lvb_ref, sgf_ref, sgb_ref, o_ref, of_ref, ob_ref, qs_ref, st_ref):
    n_heads, seq = q_ref.shape[0], q_ref.shape[1]
    nc = seq // c
    halves = _hgrn_halves(c)
    row = lax.broadcasted_iota(jnp.int32, (c, LANE), 0)
    sub = 8

    lbr = lb_ref[...]
    lbe = jnp.exp(lbr - jnp.max(lbr, axis=0, keepdims=True))
    lb_all = lbe[0] / jnp.sum(lbe, axis=0)

    st_ref[...] = jnp.zeros_like(st_ref)

    def qs_body(it, carry):
        r0 = pl.multiple_of(it * c, c)
        for hd in range(n_heads):
            qp = q_ref[hd, pl.ds(r0, c), :].astype(F32)
            qs_ref[hd, pl.ds(r0, c), :] = qp * _sigmoid(qp)
        return carry

    lax.fori_loop(0, nc, qs_body, 0)

    def gates(ch):
        hd, r0, z_ref, m_ref, _, _, reverse = ch
        d = 1 if reverse else 0
        lb = lb_all[d:d + 1, hd * LANE:(hd + 1) * LANE]
        z = z_ref[hd, pl.ds(r0, c), :].astype(F32)
        sig = _sigmoid(z)
        f = lb + (1.0 - lb) * sig
        k = (1.0 - lb) * (1.0 - sig)
        g = jnp.log2(f)
        g_hi = g.astype(BF16)
        g_lo = (g - g_hi.astype(F32)).astype(BF16)
        e = jnp.dot(m_ref[...], jnp.concatenate([g_hi, g_lo], axis=0), preferred_element_type=F32)
        return dict(f=f, k=k, e=e, b=e[0:c, :], qs=qs_ref[hd, pl.ds(r0, c), :])

    def level(ch, s, li):
        _, _, _, _, lv_ref, sg_ref, reverse = ch
        qs, k, b = s["qs"], s["k"], s["b"]
        if li < 0:
            x_q, x_k = qs.astype(BF16), k.astype(BF16)
            h = 1
        else:
            h = halves[li]
            if h == 1:
                w = jnp.where((row & 1) == (0 if reverse else 1), s["f"], 1.0)
            elif h == 2:
                w = jnp.exp2(s["e"][c:2 * c, :])
            elif h < sub:
                b3 = b.reshape(c // (2 * h), 2 * h, LANE)
                mid = h if reverse else h - 1
                w = jnp.exp2((b3 - b3[:, mid:mid + 1, :]).reshape(c, LANE) * sg_ref[li])
            else:
                parts = []
                for j in range(c // h):
                    mid = (j // 2) * 2 * h + (h if reverse else h - 1)
                    blk, ref_row = b[j * h:(j + 1) * h], b[mid:mid + 1]
                    parts.append(blk - ref_row if (j % 2 == 1) != reverse else ref_row - blk)
                w = jnp.exp2(jnp.concatenate(parts, axis=0))
            if h >= sub:
                base = jnp.concatenate(
                    [(qs if (j % 2 == 1) != reverse else k)[j * h:(j + 1) * h] for j in range(c // h)], axis=0)
            else:
                base = jnp.where((row & h) == (0 if reverse else h), qs, k)
            x_q = x_k = (base * w).astype(BF16)
        p = lax.dot_general(x_q, x_k, (((1,), (1,)), ((), ())), preferred_element_type=F32)
        a = s.get("a")
        new_a = []
        for j in range(c // sub):
            rows = slice(j * sub, (j + 1) * sub)
            if a is None:
                new_a.append(jnp.where(lv_ref[rows, :] == 0, p[rows], 0.0))
            elif h < sub or ((j * sub // h) % 2 == 1) != reverse:
                new_a.append(jnp.where(lv_ref[rows, :] == li + 1, p[rows], a[j]))
            else:
                new_a.append(a[j])
        s["a"] = new_a

    def apply(ch, s):
        hd, r0, _, _, _, _, reverse = ch
        d = 1 if reverse else 0
        qs, k, b = s["qs"], s["k"], s["b"]
        v = i_ref[hd, pl.ds(r0, c), :]
        last = 0 if reverse else c - 1
        b_last = b[last:last + 1, :]
        st = st_ref[hd, d]
        o = jnp.dot(jnp.concatenate(s["a"], axis=0).astype(BF16), v, preferred_element_type=F32)
        o = o + lax.dot_general((qs * jnp.exp2(b)).astype(BF16), st.astype(BF16), (((1,), (1,)), ((), ())),
                                preferred_element_type=F32)
        k_out = (k * jnp.exp2(b_last - b)).astype(BF16)
        st_ref[hd, d] = st * jnp.exp2(b_last) + lax.dot_general(
            v, k_out, (((0,), (0,)), ((), ())), preferred_element_type=F32)
        return o

    def scan_body(it, carry):
        rf = pl.multiple_of(it * c, c)
        rb = pl.multiple_of((nc - 1 - it) * c, c)
        chains = []
        for hd in range(n_heads):
            chains.append((hd, rf, ff_ref, mf_ref, lvf_ref, sgf_ref, False))
            chains.append((hd, rb, fb_ref, mb_ref, lvb_ref, sgb_ref, True))
        for g0 in range(0, len(chains), HG_CHAIN_GROUP):
            group = chains[g0:g0 + HG_CHAIN_GROUP]
            states = [gates(ch) for ch in group]
            for li in range(-1, len(halves)):
                for ch, s in zip(group, states):
                    level(ch, s, li)
            for ch, s in zip(group, states):
                out_ref = ob_ref if ch[6] else of_ref
                out_ref[ch[0], pl.ds(ch[1], c), :] = apply(ch, s)
        return carry

    lax.fori_loop(0, nc, scan_body, 0)

    def out_body(it, carry):
        r0 = pl.multiple_of(it * c, c)
        for hd in range(n_heads):
            tot = of_ref[hd, pl.ds(r0, c), :] + ob_ref[hd, pl.ds(r0, c), :]
            y = tot * lax.rsqrt(jnp.mean(tot * tot, axis=-1, keepdims=True) + EPS) * nw_ref[...]
            gp = g_ref[hd, pl.ds(r0, c), :].astype(F32)
            o_ref[pl.ds(r0, c), hd * LANE:(hd + 1) * LANE] = (y * (gp * _sigmoid(gp))).astype(BF16)
        return carry

    lax.fori_loop(0, nc, out_body, 0)


def _hgrn(proj, hg_lb, norm_w, n_seq, seq):
    c = HG_CHUNK
    hh = HG_HEADS_PER_STEP
    mf, lvf, sgf = _hgrn_constants(c, False)
    mb, lvb, sgb = _hgrn_constants(c, True)
    t = n_seq * seq

    def slab(first):
        return pl.BlockSpec((hh, seq, LANE), lambda b, h: (first // hh + h, b, 0))

    def const(arr):
        return pl.BlockSpec(arr.shape, lambda b, h: (0,) * arr.ndim)

    return pl.pallas_call(
        functools.partial(_hgrn_kernel, c),
        grid=(n_seq, HG_HEADS // hh),
        in_specs=[
            slab(SLAB_QH), slab(SLAB_FF), slab(SLAB_FB), slab(SLAB_IH), slab(SLAB_GH),
            pl.BlockSpec((hg_lb.shape[0], 2, hh * LANE), lambda b, h: (0, 0, h)),
            pl.BlockSpec((1, LANE), lambda b, h: (0, 0)),
            const(mf), const(mb), const(lvf), const(lvb), const(sgf), const(sgb),
        ],
        out_specs=pl.BlockSpec((seq, hh * LANE), lambda b, h: (b, h)),
        out_shape=jax.ShapeDtypeStruct((t, HG_HEADS * LANE), BF16),
        scratch_shapes=[pltpu.VMEM((hh, seq, LANE), F32)] * 3 + [pltpu.VMEM((hh, 2, LANE, LANE), F32)],
        compiler_params=_cparams("parallel", "arbitrary"),
        name="hgrn",
    )(proj, proj, proj, proj, proj, hg_lb, norm_w,
      jnp.asarray(mf, BF16), jnp.asarray(mb, BF16), jnp.asarray(lvf), jnp.asarray(lvb),
      jnp.asarray(sgf), jnp.asarray(sgb))


def _outproj_kernel(nbp, xp_ref, xs_ref, a_ref, hg_ref, wo_ref, n2_ref, wrh_ref, wrl_ref, br_ref, tri_ref,
                    x1_ref, h2_ref, ri_ref, rw_ref, cnt_ref, run_ref):
    i = pl.program_id(0)
    half = a_ref.shape[1]
    sub_rows = tri_ref.shape[0]
    n_sub = a_ref.shape[0] // sub_rows

    @pl.when(i == 0)
    def _():
        run_ref[...] = jnp.zeros_like(run_ref)

    groups = [slice(j * sub_rows, (j + 1) * sub_rows) for j in range(n_sub)]
    from_prompt = i < nbp

    x1s = []
    for rows in groups:
        acc = jnp.dot(a_ref[rows, :], wo_ref[0:half, :], preferred_element_type=F32)
        acc = acc + jnp.dot(hg_ref[rows, :], wo_ref[half:2 * half, :], preferred_element_type=F32)
        x1 = jnp.where(from_prompt, xp_ref[rows, :], xs_ref[rows, :]) + acc
        x1_ref[rows, :] = x1
        x1s.append(x1)

    logits_all = []
    for rows, x1 in zip(groups, x1s):
        h2 = x1 * lax.rsqrt(jnp.mean(x1 * x1, axis=-1, keepdims=True) + EPS) * n2_ref[...]
        for j, words in enumerate(_pack_row(h2)):
            h2_ref[pl.ds(rows.start * ROW_PLANES + j, sub_rows, stride=ROW_PLANES), :] = words
        h_hi = h2.astype(BF16)
        h_lo = (h2 - h_hi.astype(F32)).astype(BF16)
        both = jnp.dot(h_hi, wrl_ref[...], preferred_element_type=F32)
        logits_all.append((both[:, 0:LANE] + both[:, LANE:2 * LANE]
                           + jnp.dot(h_lo, wrh_ref[...], preferred_element_type=F32)) + br_ref[...])

    lane = lax.broadcasted_iota(jnp.int32, logits_all[0].shape, 1).astype(F32)
    far = float(LANE)
    is_g = lane < N_GROUPS
    picks = []
    for logits in logits_all:
        gl = jnp.where(is_g, logits, -jnp.inf)
        gmax = jnp.max(gl, axis=-1, keepdims=True)
        gidx = jnp.min(jnp.where(gl == gmax, lane, far), axis=-1, keepdims=True)
        g_w = 1.0 / jnp.sum(jnp.where(is_g, jnp.exp(logits - gmax), 0.0), axis=-1, keepdims=True)
        lo = N_GROUPS + gidx * EXPERTS_PER_GROUP
        in_grp = (lane >= lo) & (lane < lo + EXPERTS_PER_GROUP)
        el = jnp.where(in_grp, logits, -jnp.inf)
        m1 = jnp.max(el, axis=-1, keepdims=True)
        i1 = jnp.min(jnp.where(el == m1, lane, far), axis=-1, keepdims=True)
        el2 = jnp.where(lane == i1, -jnp.inf, el)
        m2 = jnp.max(el2, axis=-1, keepdims=True)
        i2 = jnp.min(jnp.where(el2 == m2, lane, far), axis=-1, keepdims=True)
        p2 = jnp.exp(m2 - m1)
        picks.append((i1, i2, g_w * (1.0 / (1.0 + p2)), g_w * (p2 / (1.0 + p2))))

    run = run_ref[...]
    for rows, (i1, i2, c1, c2) in zip(groups, picks):
        oh1 = lane == i1
        oh2 = lane == i2
        one1 = jnp.where(oh1, 1.0, 0.0)
        one2 = jnp.where(oh2, 1.0, 0.0)
        before1 = jnp.dot(tri_ref[...], one1.astype(BF16), preferred_element_type=F32)
        before2 = jnp.dot(tri_ref[...], one2.astype(BF16), preferred_element_type=F32)
        tot1 = jnp.sum(one1, axis=0, keepdims=True)
        r1 = jnp.sum(jnp.where(oh1, run + before1, 0.0), axis=-1, keepdims=True)
        r2 = jnp.sum(jnp.where(oh2, run + tot1 + before2, 0.0), axis=-1, keepdims=True)
        run = run + tot1 + jnp.sum(one2, axis=0, keepdims=True)
        ri = jnp.where(lane == 0.0, i1 - N_GROUPS, jnp.where(lane == 1.0, i2 - N_GROUPS,
                       jnp.where(lane == 2.0, r1, jnp.where(lane == 3.0, r2, 0.0))))
        ri_ref[rows, :] = ri.astype(jnp.int32)
        rw_ref[rows, :] = jnp.where(lane == 0.0, c1, jnp.where(lane == 1.0, c2, 0.0))
    run_ref[...] = run
    cnt_ref[...] = run.astype(jnp.int32)


def _outproj(xp, xs, attn, hg, wo_bf16, norm2_w, wr_hi, wr_lo, br):
    tp, ts = xp.shape[0], xs.shape[0]
    t = tp + ts
    tm = _largest_divisor(np.gcd(tp, ts), ROUTE_TILE)
    nbp, nbs = tp // tm, ts // tm
    half = attn.shape[1]
    sub_rows = _largest_divisor(tm, ROUTE_SUB)
    strict_lower = jnp.asarray(np.tril(np.ones((sub_rows, sub_rows), np.float32), -1), BF16)

    def const(shape):
        return pl.BlockSpec(shape, lambda i: (0, 0), pipeline_mode=pl.Buffered(1))

    def rows(width):
        return pl.BlockSpec((tm, width), lambda i: (i, 0))

    return pl.pallas_call(
        functools.partial(_outproj_kernel, nbp),
        grid=(nbp + nbs,),
        in_specs=[
            pl.BlockSpec((tm, D_MODEL), lambda i: (jnp.minimum(i, nbp - 1), 0)),
            pl.BlockSpec((tm, D_MODEL), lambda i: (jnp.maximum(i - nbp, 0), 0)),
            rows(half), rows(half),
            const((2 * half, D_MODEL)), const((1, D_MODEL)),
            const((D_MODEL, LANE)), const((D_MODEL, 2 * LANE)), const((1, LANE)), const((sub_rows, sub_rows)),
        ],
        out_specs=[rows(D_MODEL), pl.BlockSpec((tm * ROW_PLANES, LANE), lambda i: (i, 0)),
                   rows(LANE), rows(LANE), const((1, LANE))],
        out_shape=[
            jax.ShapeDtypeStruct((t, D_MODEL), F32),
            jax.ShapeDtypeStruct((t * ROW_PLANES, LANE), U32),
            jax.ShapeDtypeStruct((t, LANE), jnp.int32),
            jax.ShapeDtypeStruct((t, LANE), F32),
            jax.ShapeDtypeStruct((1, LANE), jnp.int32),
        ],
        scratch_shapes=[pltpu.VMEM((1, LANE), F32)],
        compiler_params=_cparams("arbitrary"),
        name="outproj",
    )(xp, xs, attn, hg, wo_bf16, norm2_w, wr_hi, wr_lo, br, strict_lower)


def _plan_expert_tiles(ri, counts, tm):
    t = ri.shape[0]
    n_slots = 2 * t
    n_tiles = n_slots // tm + N_EXPERTS
    p = n_tiles * tm
    ef = jnp.concatenate([ri[:, 0], ri[:, 1]])
    rank = jnp.concatenate([ri[:, 2], ri[:, 3]])
    tiles_per = (counts + tm - 1) // tm
    tile_end = jnp.cumsum(tiles_per)
    seg_start = (tile_end - tiles_per) * tm
    experts = jnp.arange(N_EXPERTS, dtype=jnp.int32)
    ppos = jnp.sum(jnp.where(ef[:, None] == experts[None, :], seg_start[None, :], 0), axis=1) + rank
    slot = jnp.arange(n_slots, dtype=jnp.int32)
    real_dest = jnp.full((p,), -1, jnp.int32).at[ppos].set(slot, unique_indices=True)
    is_pad = real_dest < 0
    pos = jnp.arange(p, dtype=jnp.int32)
    dest = jnp.where(is_pad, n_slots + pos % (2 * tm), real_dest)
    src_tok = jnp.where(is_pad, 0, real_dest % t)
    tiles = jnp.arange(n_tiles, dtype=jnp.int32)
    tile_expert = jnp.minimum(jnp.sum((tile_end[None, :] <= tiles[:, None]).astype(jnp.int32), axis=1),
                              N_EXPERTS - 1)
    n_used = tile_end[N_EXPERTS - 1:].astype(jnp.int32)
    return src_tok.reshape(n_tiles, 1, tm), dest.reshape(n_tiles, 1, tm), tile_expert, n_used, n_tiles


def _experts_kernel(te_ref, nu_ref, src_ref, srcn_ref, dstp_ref, dst_ref, h2_hbm, wg_ref, wu_ref, wd_ref, y_hbm,
                    gbuf, obuf, wg_bf, wu_bf, wd_bf, gsem, ssem):
    i = pl.program_id(0)
    n = nu_ref[0]
    pitch = EXPERT_ROW_PITCH
    n_planes = ROW_PLANES
    tm = gbuf.shape[1] // pitch
    slot = i % 2
    other = 1 - slot
    live = i < n

    def gather_row(idx_ref, k, s):
        src = h2_hbm.at[pl.ds(pl.multiple_of(idx_ref[0, 0, k], n_planes), n_planes), :]
        return pltpu.make_async_copy(src, gbuf.at[s, pl.ds(k * pitch, n_planes), :], gsem.at[s])

    def scatter_row(idx_ref, k, s):
        dst = y_hbm.at[pl.ds(pl.multiple_of(idx_ref[0, 0, k], n_planes), n_planes), :]
        return pltpu.make_async_copy(obuf.at[s, pl.ds(k * pitch, n_planes), :], dst, ssem.at[s])

    def wait_gather(s):
        pltpu.make_async_copy(h2_hbm.at[pl.ds(0, tm * n_planes), :], gbuf.at[s, pl.ds(0, tm * n_planes), :],
                              gsem.at[s]).wait()

    def wait_scatter(s):
        pltpu.make_async_copy(obuf.at[s, pl.ds(0, tm * n_planes), :], y_hbm.at[pl.ds(0, tm * n_planes), :],
                              ssem.at[s]).wait()

    @pl.when(i == 0)
    def _():
        def body(k, carry):
            gather_row(src_ref, k, 0).start()
            return carry
        lax.fori_loop(0, tm, body, 0, unroll=8)
        obuf[1] = jnp.zeros(obuf.shape[1:], obuf.dtype)
        spare0 = y_hbm.shape[0] - 2 * tm * n_planes
        for part in range(2):
            fill = pltpu.make_async_copy(
                obuf.at[1, pl.ds(0, tm * n_planes), :],
                y_hbm.at[pl.ds(spare0 + part * tm * n_planes, tm * n_planes), :], ssem.at[0])
            fill.start()
            fill.wait()

    @pl.when((i >= 1) & live)
    def _():
        wait_scatter(slot)

    changed = live & ((i == 0) | (te_ref[i] != te_ref[jnp.maximum(i - 1, 0)]))

    @pl.when(changed)
    def _():
        wg_bf[...] = wg_ref[0].astype(BF16)
        wu_bf[...] = wu_ref[0].astype(BF16)
        wd_bf[...] = wd_ref[0].astype(BF16)

    def tile_body(s):
        wait_gather(s)
        words = [gbuf[s, pl.ds(j, tm, stride=pitch), :] for j in range(n_planes)]
        hb = jnp.concatenate([p.astype(BF16) for p in _unpack_row(words)], axis=1)
        for k in range(tm):
            gather_row(srcn_ref, k, 1 - s).start(priority=k % 2)
        for k in range(tm):
            scatter_row(dstp_ref, k, 1 - s).start(priority=(k + 1) % 2)
        gate = jnp.dot(hb, wg_bf[...], preferred_element_type=F32)
        up = jnp.dot(hb, wu_bf[...], preferred_element_type=F32)
        act = (gate * _sigmoid(gate) * up).astype(BF16)
        y = jnp.dot(act, wd_bf[...], preferred_element_type=F32)
        for j, out_words in enumerate(_pack_row(y)):
            obuf[s, pl.ds(j, tm, stride=pitch), :] = out_words

    @pl.when((slot == 0) & live)
    def _():
        tile_body(0)

    @pl.when((slot == 1) & live)
    def _():
        tile_body(1)

    @pl.when(i == n - 1)
    def _():
        wait_gather(other)
        wait_scatter(other)

        def body(k, carry):
            scatter_row(dst_ref, k, slot).start()
            return carry
        lax.fori_loop(0, tm, body, 0, unroll=8)
        wait_scatter(slot)


def _experts(h2, src_tok, dest, tile_expert, n_used, n_tiles, w_gate, w_up, w_down):
    tm = EXPERT_TILE
    n_planes = ROW_PLANES
    t = h2.shape[0] // n_planes
    src_tok, dest = src_tok * n_planes, dest * n_planes

    def idx_spec(shift):
        return pl.BlockSpec((1, 1, tm), lambda i, te, nu: (jnp.minimum(i + shift, nu[0] - 1), 0, 0),
                            memory_space=pltpu.SMEM)

    def w_spec(shape):
        return pl.BlockSpec((1,) + shape, lambda i, te, nu: (te[jnp.minimum(i, nu[0] - 1)], 0, 0))

    spare = ((2 * t + jnp.arange(tm, dtype=jnp.int32)) * n_planes).reshape(1, 1, tm)
    dest_prev = jnp.concatenate([spare, dest[:-1]], axis=0)

    grid_spec = pltpu.PrefetchScalarGridSpec(
        num_scalar_prefetch=2,
        grid=(n_tiles,),
        in_specs=[
            idx_spec(0), idx_spec(1), idx_spec(0), idx_spec(0),
            pl.BlockSpec(memory_space=pl.ANY),
            w_spec((D_MODEL, D_EXPERT)), w_spec((D_MODEL, D_EXPERT)), w_spec((D_EXPERT, D_MODEL)),
        ],
        out_specs=pl.BlockSpec(memory_space=pl.ANY),
        scratch_shapes=[
            pltpu.VMEM((2, tm * EXPERT_ROW_PITCH, LANE), U32),
            pltpu.VMEM((2, tm * EXPERT_ROW_PITCH, LANE), U32),
            pltpu.VMEM((D_MODEL, D_EXPERT), BF16),
            pltpu.VMEM((D_MODEL, D_EXPERT), BF16),
            pltpu.VMEM((D_EXPERT, D_MODEL), BF16),
            pltpu.SemaphoreType.DMA((2,)),
            pltpu.SemaphoreType.DMA((2,)),
        ],
    )
    return pl.pallas_call(
        _experts_kernel,
        grid_spec=grid_spec,
        out_shape=jax.ShapeDtypeStruct(((2 * t + 2 * tm) * n_planes, LANE), U32),
        compiler_params=_cparams("arbitrary"),
        name="experts",
    )(tile_expert, n_used, src_tok, src_tok, dest_prev, dest, h2, w_gate, w_up, w_down)


def _final_kernel(x1_ref, ya_ref, yb_ref, rw_ref, nw_ref, o_ref):
    rw = rw_ref[...]
    c1, c2 = rw[:, 0:1], rw[:, 1:2]
    tm = x1_ref.shape[0]
    ya = _unpack_row([ya_ref[pl.ds(j, tm, stride=ROW_PLANES), :] for j in range(ROW_PLANES)])
    yb = _unpack_row([yb_ref[pl.ds(j, tm, stride=ROW_PLANES), :] for j in range(ROW_PLANES)])
    planes = [x1_ref[:, j * LANE:(j + 1) * LANE] + c1 * ya[j] + c2 * yb[j] for j in range(len(ya))]
    sq = planes[0] * planes[0]
    for xj in planes[1:]:
        sq = sq + xj * xj
    r = lax.rsqrt(jnp.sum(sq, axis=-1, keepdims=True) * (1.0 / o_ref.shape[1]) + EPS)
    for j, xj in enumerate(planes):
        cols = slice(j * LANE, (j + 1) * LANE)
        o_ref[:, cols] = xj * r * nw_ref[:, cols]


def _final(x1, y, rw, norm_w, row0, n_rows):
    t = x1.shape[0]
    tm = _largest_divisor(np.gcd(np.gcd(row0, n_rows), t), 512)
    b0 = row0 // tm
    bt = t // tm

    return pl.pallas_call(
        _final_kernel,
        grid=(n_rows // tm,),
        in_specs=[
            pl.BlockSpec((tm, D_MODEL), lambda i: (b0 + i, 0)),
            pl.BlockSpec((tm * ROW_PLANES, LANE), lambda i: (b0 + i, 0)),
            pl.BlockSpec((tm * ROW_PLANES, LANE), lambda i: (bt + b0 + i, 0)),
            pl.BlockSpec((tm, LANE), lambda i: (b0 + i, 0)),
            pl.BlockSpec((1, D_MODEL), lambda i: (0, 0)),
        ],
        out_specs=pl.BlockSpec((tm, D_MODEL), lambda i: (i, 0)),
        out_shape=jax.ShapeDtypeStruct((n_rows, D_MODEL), F32),
        compiler_params=_cparams("parallel"),
        name="final",
    )(x1, y, y, rw, norm_w)


def kernel(x_prompt, x_sample, norm1_w, w_in, attn_sink, hg_lb, hg_norm_w, w_out, norm2_w, w_router_g,
           b_router_g, w_router_e, b_router_e, w_gate, w_up, w_down, final_norm_w):
    bp, seq, d = x_prompt.shape
    bs, seq_s, _ = x_sample.shape
    assert d == D_MODEL and seq == seq_s and seq % max(ATTN_BLOCK, HG_CHUNK) == 0
    assert w_in.shape[0] == 1 and w_in.shape[2] == N_SLABS * LANE
    n_seq = bp + bs
    tp, ts = bp * seq, bs * seq
    xp = x_prompt.reshape(tp, d)
    xs = x_sample.reshape(ts, d)

    proj = _inproj(xp, xs, norm1_w, w_in[0].astype(BF16))
    attn = _attention(proj, attn_sink, n_seq, seq)
    hg = _hgrn(proj, hg_lb, hg_norm_w, n_seq, seq)

    wr = jnp.concatenate([w_router_g[0], w_router_e[0].reshape(d, N_EXPERTS)], axis=1)
    wr = jnp.pad(wr, ((0, 0), (0, LANE - wr.shape[1])))
    wr_hi = wr.astype(BF16)
    wr_lo = jnp.concatenate([wr_hi, (wr - wr_hi.astype(F32)).astype(BF16)], axis=1)
    br = jnp.concatenate([b_router_g[0], b_router_e[0].reshape(N_EXPERTS)])
    br = jnp.pad(br, (0, LANE - br.shape[0])).reshape(1, LANE)
    x1, h2, ri, rw, cnt = _outproj(xp, xs, attn, hg, w_out[0].astype(BF16), norm2_w, wr_hi, wr_lo, br)

    counts = cnt[0, N_GROUPS:N_GROUPS + N_EXPERTS]
    src_tok, dest, tile_expert, n_used, n_tiles = _plan_expert_tiles(ri, counts, EXPERT_TILE)
    y = _experts(h2, src_tok, dest, tile_expert, n_used, n_tiles, w_gate[0], w_up[0], w_down[0])

    fnw = final_norm_w.reshape(1, d)
    y_prompt = _final(x1, y, rw, fnw, 0, tp).reshape(bp, seq, d)
    y_sample = _final(x1, y, rw, fnw, tp, ts).reshape(bs, seq, d)
    return (y_prompt, y_sample)
```
